```python
import jax, jax.numpy as jnp
from jax import lax
import numpy as np

D_MODEL = 2048
BATCH = 1
SEQ = 16384
DEPTH = 2
DEC_BATCH = 8
DEC_SEQ = 4096
PAST_LEN = 128

HEAD_DIM = 64
GRID_W = 64
Q_BLOCK = 128
ROPE_THETA = 10000.0
NORM_EPS = 1e-6
NEG_INF = -1e30
A_Q_HEADS = 12
A_KV_HEADS = 4
B_HEADS = 12
B_WINDOWS = (128, 512, 2048)
B_DILATIONS = (1, 4, 16)
ALIBI_MAX_BIAS = 8.0
C_HEADS = 8
C_NOPE = 64
C_ROPE = 32
C_V = 64
C_Q_LORA = 512
C_KV_LORA = 256
A_W = A_Q_HEADS * HEAD_DIM
A_KV_W = A_KV_HEADS * HEAD_DIM
B_W = B_HEADS * HEAD_DIM
C_W = C_HEADS * C_V
MIX_WIDTH = A_W + B_W + C_W
IN_SPLIT_SIZES = (A_W, A_KV_W, A_KV_W, B_W, B_W, B_W, C_Q_LORA, C_KV_LORA, C_ROPE)
IN_WIDTH = A_W + 2 * A_KV_W + 3 * B_W + C_Q_LORA + C_KV_LORA + C_ROPE
D_FF = 5632
CONV_W = 3

kernel_name = "hybrid_bidir_encoder_gqa_dilated_mla"


def rms_norm(x, g):
    x32 = x.astype(jnp.float32)
    y = x32 * lax.rsqrt(jnp.mean(x32 * x32, axis=-1, keepdims=True) + NORM_EPS)
    return (y * g.astype(jnp.float32)).astype(x.dtype)


def layer_norm(x, g, b):
    x32 = x.astype(jnp.float32)
    mu = jnp.mean(x32, axis=-1, keepdims=True)
    var = jnp.mean(jnp.square(x32 - mu), axis=-1, keepdims=True)
    y = (x32 - mu) * lax.rsqrt(var + NORM_EPS)
    return (y * g.astype(jnp.float32) + b.astype(jnp.float32)).astype(x.dtype)


def rope_angles(pos, dim):
    inv_freq = jnp.power(ROPE_THETA, -jnp.arange(0, dim, 2, dtype=jnp.float32) / dim)
    ang = pos.astype(jnp.float32)[:, None] * inv_freq[None, :]
    return jnp.cos(ang), jnp.sin(ang)


def apply_rope(x, cos, sin):
    half = x.shape[-1] // 2
    x32 = x.astype(jnp.float32)
    x1, x2 = x32[..., :half], x32[..., half:]
    c = cos[None, :, None, :]
    s = sin[None, :, None, :]
    return jnp.concatenate([x1 * c - x2 * s, x2 * c + x1 * s], axis=-1).astype(x.dtype)


def position_tables(S):
    rows = S // GRID_W
    row = jnp.repeat(jnp.arange(rows, dtype=jnp.int32), GRID_W)
    col = jnp.tile(jnp.arange(GRID_W, dtype=jnp.int32), rows)
    t = jnp.arange(S, dtype=jnp.int32)
    half = HEAD_DIM // 2
    cr, sr = rope_angles(row, half)
    cc, sc = rope_angles(col, half)
    c1, s1 = rope_angles(t, C_ROPE)
    return cr, sr, cc, sc, c1, s1


def axial_rope(x, cr, sr, cc, sc):
    half = x.shape[-1] // 2
    return jnp.concatenate([apply_rope(x[..., :half], cr, sr), apply_rope(x[..., half:], cc, sc)], axis=-1)


def blocked_dense_attention(q, k, v, scale):
    B, S, Hq, Dk = q.shape
    Hkv, Dv = k.shape[2], v.shape[-1]
    G = Hq // Hkv
    nb = S // Q_BLOCK
    qb = q.reshape(B, nb, Q_BLOCK, Hkv, G, Dk).transpose(1, 0, 2, 3, 4, 5)

    def one_block(qblk):
        s = jnp.einsum('bqhgd,bkhd->bhgqk', qblk, k, preferred_element_type=jnp.float32) * scale
        p = jax.nn.softmax(s, axis=-1)
        return jnp.einsum('bhgqk,bkhd->bqhgd', p.astype(v.dtype), v)

    out = lax.map(one_block, qb)
    return out.transpose(1, 0, 2, 3, 4, 5).reshape(B, S, Hq * Dv)


def dilated_branch(q, k, v, slopes, window, dilation, scale):
    B, S, H, D = q.shape
    half = window // (2 * dilation)
    blk = half
    L = S // dilation
    nblk = -(-L // blk)
    Lp = nblk * blk

    def to_sub(t):
        t = t.reshape(B, L, dilation, H, D).transpose(0, 2, 1, 3, 4)
        return jnp.pad(t, ((0, 0), (0, 0), (0, Lp - L), (0, 0), (0, 0)))

    def windows(t):
        tp = jnp.pad(t, ((0, 0), (0, 0), (blk, blk), (0, 0), (0, 0))).reshape(B, dilation, nblk + 2, blk, H, D)
        return jnp.concatenate([tp[:, :, :-2], tp[:, :, 1:-1], tp[:, :, 2:]], axis=3)

    qb = to_sub(q).reshape(B, dilation, nblk, blk, H, D)
    kw = windows(to_sub(k))
    vw = windows(to_sub(v))
    s = jnp.einsum('brnqhd,brnkhd->brnhqk', qb, kw, preferred_element_type=jnp.float32) * scale
    qi = jnp.arange(blk)[:, None]
    kj = jnp.arange(3 * blk)[None, :]
    rel = kj - blk - qi
    key_idx = jnp.arange(nblk)[:, None, None] * blk + (kj - blk)[None]
    valid = (jnp.abs(rel) <= half)[None] & (key_idx >= 0) & (key_idx < L)
    dist = (jnp.abs(rel) * dilation).astype(jnp.float32)
    bias = -slopes[:, None, None] * dist[None]
    s = jnp.where(valid[None, None, :, None], s + bias[None, None, None], NEG_INF)
    lse = jax.nn.logsumexp(s, axis=-1)
    p = jnp.exp(s - lse[..., None])
    o = jnp.einsum('brnhqk,brnkhd->brnqhd', p.astype(v.dtype), vw)
    o = o.reshape(B, dilation, Lp, H, D)[:, :, :L].transpose(0, 2, 1, 3, 4).reshape(B, S, H, D)
    lse = lse.transpose(0, 1, 2, 4, 3).reshape(B, dilation, Lp, H)[:, :, :L]
    lse = lse.transpose(0, 2, 1, 3).reshape(B, S, H)
    return o, lse


def token_mixer(x, tables, w_in, a_q_norm, a_k_norm, c_q_norm, c_kv_norm, w_uq, w_ukv, w_o):
    B, S, _ = x.shape
    cr, sr, cc, sc, c1, s1 = tables
    h = x @ w_in
    points = np.cumsum(np.array(IN_SPLIT_SIZES))[:-1].tolist()
    aq, ak, av, bq, bk, bv, cq, ckv, ckr = jnp.split(h, points, axis=-1)

    aq = axial_rope(rms_norm(aq.reshape(B, S, A_Q_HEADS, HEAD_DIM), a_q_norm), cr, sr, cc, sc)
    ak = axial_rope(rms_norm(ak.reshape(B, S, A_KV_HEADS, HEAD_DIM), a_k_norm), cr, sr, cc, sc)
    av = av.reshape(B, S, A_KV_HEADS, HEAD_DIM)
    a_out = blocked_dense_attention(aq, ak, av, HEAD_DIM ** -0.5)

    bq = bq.reshape(B, S, B_HEADS, HEAD_DIM)
    bk = bk.reshape(B, S, B_HEADS, HEAD_DIM)
    bv = bv.reshape(B, S, B_HEADS, HEAD_DIM)
    slopes = jnp.exp2(-ALIBI_MAX_BIAS * jnp.arange(1, B_HEADS + 1, dtype=jnp.float32) / B_HEADS)
    outs, lses = [], []
    for window, dilation in zip(B_WINDOWS, B_DILATIONS):
        o_i, l_i = dilated_branch(bq, bk, bv, slopes, window, dilation, HEAD_DIM ** -0.5)
        outs.append(o_i.astype(jnp.float32))
        lses.append(l_i)
    wts = jax.nn.softmax(jnp.stack(lses), axis=0)
    b_out = jnp.sum(wts[..., None] * jnp.stack(outs), axis=0).astype(x.dtype).reshape(B, S, B_W)

    q_c = (rms_norm(cq, c_q_norm) @ w_uq).reshape(B, S, C_HEADS, C_NOPE + C_ROPE)
    q_nope, q_rope = q_c[..., :C_NOPE], apply_rope(q_c[..., C_NOPE:], c1, s1)
    kv_c = (rms_norm(ckv, c_kv_norm) @ w_ukv).reshape(B, S, C_HEADS, C_NOPE + C_V)
    k_nope, v_c = kv_c[..., :C_NOPE], kv_c[..., C_NOPE:]
    k_rope = jnp.broadcast_to(apply_rope(ckr[:, :, None, :], c1, s1), (B, S, C_HEADS, C_ROPE))
    qm = jnp.concatenate([q_nope, q_rope], axis=-1)
    km = jnp.concatenate([k_nope, k_rope], axis=-1)
    c_out = blocked_dense_attention(qm, km, v_c, (C_NOPE + C_ROPE) ** -0.5)

    return jnp.concatenate([a_out, b_out, c_out], axis=-1) @ w_o


def conv_ffn(x, w_up, w_conv, b_conv, w_down):
    S = x.shape[1]
    u = x @ w_up
    g, val = u[..., :D_FF], u[..., D_FF:]
    pad = CONV_W // 2
    gp = jnp.pad(g, ((0, 0), (pad, pad), (0, 0)))
    gc = sum(gp[:, j:j + S] * w_conv[j] for j in range(CONV_W)) + b_conv
    return (jax.nn.gelu(gc, approximate=False) * val) @ w_down


def trunk(x, ln_emb_g, ln_emb_b, w_in, a_q_norm, a_k_norm, c_q_norm, c_kv_norm, w_uq, w_ukv, w_o,
          ln1_g, ln1_b, w_up, w_conv, b_conv, w_down, ln2_g, ln2_b):
    alpha = (2.0 * DEPTH) ** 0.25
    tables = position_tables(x.shape[1])
    x = layer_norm(x, ln_emb_g, ln_emb_b)
    for l in range(DEPTH):
        mix = token_mixer(x, tables, w_in[l], a_q_norm[l], a_k_norm[l], c_q_norm[l], c_kv_norm[l],
                          w_uq[l], w_ukv[l], w_o[l])
        x = layer_norm(alpha * x + mix, ln1_g[l], ln1_b[l])
        ff = conv_ffn(x, w_up[l], w_conv[l], b_conv[l], w_down[l])
        x = layer_norm(alpha * x + ff, ln2_g[l], ln2_b[l])
    return x


def setup_inputs(seed: int = 0) -> dict:
    key = jax.random.key(seed)
    ks = jax.random.split(key, 24)
    beta = (8.0 * DEPTH) ** -0.25
    f32 = jnp.float32

    def nrm(k, shape, scale):
        return jax.random.normal(k, shape, f32) * scale

    def gain(k, shape):
        return 1.0 + 0.1 * jax.random.normal(k, shape, f32)

    return {
        "x_prompt": jax.random.normal(ks[0], (BATCH, SEQ, D_MODEL), f32),
        "x_sample": jax.random.normal(ks[1], (DEC_BATCH, DEC_SEQ, D_MODEL), f32),
        "ln_emb_g": gain(ks[2], (D_MODEL,)),
        "ln_emb_b": nrm(ks[3], (D_MODEL,), 0.01),
        "w_in": nrm(ks[4], (DEPTH, D_MODEL, IN_WIDTH), D_MODEL ** -0.5),
        "a_q_norm": gain(ks[5], (DEPTH, HEAD_DIM)),
        "a_k_norm": gain(ks[6], (DEPTH, HEAD_DIM)),
        "c_q_norm": gain(ks[7], (DEPTH, C_Q_LORA)),
        "c_kv_norm": gain(ks[8], (DEPTH, C_KV_LORA)),
        "w_uq": nrm(ks[9], (DEPTH, C_Q_LORA, C_HEADS * (C_NOPE + C_ROPE)), C_Q_LORA ** -0.5),
        "w_ukv": nrm(ks[10], (DEPTH, C_KV_LORA, C_HEADS * (C_NOPE + C_V)), C_KV_LORA ** -0.5),
        "w_o": nrm(ks[11], (DEPTH, MIX_WIDTH, D_MODEL), beta * MIX_WIDTH ** -0.5),
        "ln1_g": gain(ks[12], (DEPTH, D_MODEL)),
        "ln1_b": nrm(ks[13], (DEPTH, D_MODEL), 0.01),
        "w_up": nrm(ks[14], (DEPTH, D_MODEL, 2 * D_FF), D_MODEL ** -0.5),
        "w_conv": nrm(ks[15], (DEPTH, CONV_W, D_FF), CONV_W ** -0.5),
        "b_conv": nrm(ks[16], (DEPTH, D_FF), 0.01),
        "w_down": nrm(ks[17], (DEPTH, D_FF, D_MODEL), beta * D_FF ** -0.5),
        "ln2_g": gain(ks[18], (DEPTH, D_MODEL)),
        "ln2_b": nrm(ks[19], (DEPTH, D_MODEL), 0.01),
    }


def reference(x_prompt, x_sample, ln_emb_g, ln_emb_b, w_in, a_q_norm, a_k_norm, c_q_norm, c_kv_norm,
              w_uq, w_ukv, w_o, ln1_g, ln1_b, w_up, w_conv, b_conv, w_down, ln2_g, ln2_b):
    y_prompt = trunk(x_prompt, ln_emb_g, ln_emb_b, w_in, a_q_norm, a_k_norm, c_q_norm, c_kv_norm, w_uq, w_ukv,
                     w_o, ln1_g, ln1_b, w_up, w_conv, b_conv, w_down, ln2_g, ln2_b)
    y_sample = trunk(x_sample, ln_emb_g, ln_emb_b, w_in, a_q_norm, a_k_norm, c_q_norm, c_kv_norm, w_uq, w_ukv,
                     w_o, ln1_g, ln1_b, w_up, w_conv, b_conv, w_down, ln2_g, ln2_b)
    return (y_prompt, y_sample)
```

```python
import functools
import math

import numpy as np
import jax
import jax.numpy as jnp
from jax import lax
from jax.experimental import pallas as pl
from jax.experimental.pallas import tpu as pltpu

F32 = jnp.float32
BF16 = jnp.bfloat16

D_MODEL = 2048
DEPTH = 2
HEAD_DIM = 64
GRID_W = 64
ROPE_THETA = 10000.0
NORM_EPS = 1e-6
NEG_INF = -1e30
A_Q_HEADS = 12
A_KV_HEADS = 4
A_GROUP = A_Q_HEADS // A_KV_HEADS
B_HEADS = 12
B_WINDOWS = (128, 512, 2048)
B_DILATIONS = (1, 4, 16)
ALIBI_MAX_BIAS = 8.0
C_HEADS = 8
C_NOPE = 64
C_ROPE = 32
C_V = 64
C_Q_LORA = 512
C_KV_LORA = 256
A_W = A_Q_HEADS * HEAD_DIM
A_KV_W = A_KV_HEADS * HEAD_DIM
B_W = B_HEADS * HEAD_DIM
C_W = C_HEADS * C_V
D_FF = 5632
CONV_W = 3
ALPHA = (2.0 * DEPTH) ** 0.25

ROPE_FREQS = 16
V_ROWS = HEAD_DIM + 16
C_QK = 128

VMEM_LIMIT_BYTES = 56 * 1024 * 1024

TM = 512
TK = 512
TQ_A = 512
TQ_C = 1024
BQ = 256
B_REACH = 4
TF = 512


def _params(sem):
    return pltpu.CompilerParams(dimension_semantics=sem, vmem_limit_bytes=VMEM_LIMIT_BYTES)


def _dot(a, b):
    return jnp.dot(a, b, preferred_element_type=F32)


def _layer_norm(y, g, b):
    mu = jnp.mean(y, axis=-1, keepdims=True)
    yc = y - mu
    var = jnp.mean(yc * yc, axis=-1, keepdims=True)
    return yc * lax.rsqrt(var + NORM_EPS) * g + b


def _value_rows(vT):
    n = vT.shape[1]
    row = lax.broadcasted_iota(jnp.int32, (V_ROWS - HEAD_DIM, n), 0)
    ones = jnp.where(row == 0, 1.0, 0.0).astype(F32)
    return jnp.concatenate([vT, ones], axis=0).astype(BF16)


def _rope_rows(x1, x2, c, s):
    return x1 * c - x2 * s, x2 * c + x1 * s


def _ln_kernel(x_ref, g_ref, b_ref, of_ref, ob_ref):
    y = _layer_norm(x_ref[...], g_ref[...], b_ref[...])
    of_ref[...] = y
    ob_ref[...] = y.astype(BF16)


def _embed_ln(x, g, b):
    n = x.shape[0]
    row = pl.BlockSpec((TM, D_MODEL), lambda i: (i, 0))
    vec = pl.BlockSpec((1, D_MODEL), lambda i: (0, 0))
    return pl.pallas_call(
        _ln_kernel,
        grid=(n // TM,),
        in_specs=[row, vec, vec],
        out_specs=[row, row],
        out_shape=[jax.ShapeDtypeStruct((n, D_MODEL), F32), jax.ShapeDtypeStruct((n, D_MODEL), BF16)],
        compiler_params=_params(("parallel",)),
        name="embed_ln",
    )(x, g.reshape(1, -1), b.reshape(1, -1))


def _proj_a_kernel(x_ref, w_ref, gq_ref, gk_ref, cr_ref, sr_ref, cc_ref, sc_ref, q_ref, k_ref, v_ref):
    h = _dot(x_ref[...], w_ref[...])
    hT = h.T
    cr, sr, cc, sc = cr_ref[...], sr_ref[...], cc_ref[...], sc_ref[...]

    def norm_rope(blk, g):
        ms = jnp.mean(blk * blk, axis=0, keepdims=True)
        y = blk * lax.rsqrt(ms + NORM_EPS) * g
        r1, r2 = _rope_rows(y[0:16], y[16:32], cr, sr)
        r3, r4 = _rope_rows(y[32:48], y[48:64], cc, sc)
        return jnp.concatenate([r1, r2, r3, r4], axis=0)

    gq, gk = gq_ref[...], gk_ref[...]
    zeros = jnp.zeros((A_KV_W, TM), BF16)
    for hq in range(A_Q_HEADS):
        kvh = hq // A_GROUP
        qh = (norm_rope(hT[hq * HEAD_DIM:(hq + 1) * HEAD_DIM], gq) * (HEAD_DIM ** -0.5)).astype(BF16)
        q_ref[0, hq] = zeros
        q_ref[0, hq, kvh * HEAD_DIM:(kvh + 1) * HEAD_DIM, :] = qh
    kT = jnp.concatenate(
        [norm_rope(hT[A_W + j * HEAD_DIM:A_W + (j + 1) * HEAD_DIM], gk) for j in range(A_KV_HEADS)], axis=0)
    k_ref[0] = kT.T.astype(BF16)
    v0 = A_W + A_KV_W
    for j in range(A_KV_HEADS):
        v_ref[0, j, 0] = _value_rows(hT[v0 + j * HEAD_DIM:v0 + (j + 1) * HEAD_DIM])


def _proj_a(xb, w_a, gq, gk, tabs, B, S):
    cr, sr, cc, sc = tabs
    nt = S // TM
    tab = pl.BlockSpec((ROPE_FREQS, TM), lambda b, i: (0, i))
    gain = pl.BlockSpec((HEAD_DIM, 1), lambda b, i: (0, 0))
    return pl.pallas_call(
        _proj_a_kernel,
        grid=(B, nt),
        in_specs=[
            pl.BlockSpec((TM, D_MODEL), lambda b, i: (b * nt + i, 0)),
            pl.BlockSpec(w_a.shape, lambda b, i: (0, 0)),
            gain, gain, tab, tab, tab, tab,
        ],
        out_specs=[
            pl.BlockSpec((1, A_Q_HEADS, A_KV_W, TM), lambda b, i: (b, 0, 0, i)),
            pl.BlockSpec((1, TM, A_KV_W), lambda b, i: (b, i, 0)),
            pl.BlockSpec((1, A_KV_HEADS, 1, V_ROWS, TM), lambda b, i: (b, 0, i, 0, 0)),
        ],
        out_shape=[
            jax.ShapeDtypeStruct((B, A_Q_HEADS, A_KV_W, S), BF16),
            jax.ShapeDtypeStruct((B, S, A_KV_W), BF16),
            jax.ShapeDtypeStruct((B, A_KV_HEADS, S // TK, V_ROWS, TK), BF16),
        ],
        compiler_params=_params(("parallel", "parallel")),
        name="proj_a",
    )(xb, w_a, gq.reshape(-1, 1), gk.reshape(-1, 1), cr, sr, cc, sc)


def _proj_b_kernel(x_ref, w_ref, q_ref, k_ref, v_ref):
    h = _dot(x_ref[...], w_ref[...])
    hq = h[:, 0:B_W]
    qT = (hq * (HEAD_DIM ** -0.5)).T.astype(BF16)
    zeros = jnp.zeros((HEAD_DIM, TM), BF16)
    for hd in range(B_HEADS):
        lo = (hd % 2) * HEAD_DIM
        q_ref[0, hd, lo:lo + HEAD_DIM, :] = qT[hd * HEAD_DIM:(hd + 1) * HEAD_DIM]
        q_ref[0, hd, HEAD_DIM - lo:2 * HEAD_DIM - lo, :] = zeros
    for p in range(B_HEADS // 2):
        k_ref[0, p] = h[:, B_W + p * 128:B_W + (p + 1) * 128].astype(BF16)
    vT = h[:, 2 * B_W:3 * B_W].T
    for hd in range(B_HEADS):
        rows = _value_rows(vT[hd * HEAD_DIM:(hd + 1) * HEAD_DIM])
        for c in range(TM // BQ):
            v_ref[0, hd, c] = rows[:, c * BQ:(c + 1) * BQ]


def _proj_b(xb, w_b, B, S):
    nt = S // TM
    return pl.pallas_call(
        _proj_b_kernel,
        grid=(B, nt),
        in_specs=[
            pl.BlockSpec((TM, D_MODEL), lambda b, i: (b * nt + i, 0)),
            pl.BlockSpec(w_b.shape, lambda b, i: (0, 0)),
        ],
        out_specs=[
            pl.BlockSpec((1, B_HEADS, 2 * HEAD_DIM, TM), lambda b, i: (b, 0, 0, i)),
            pl.BlockSpec((1, B_HEADS // 2, TM, 2 * HEAD_DIM), lambda b, i: (b, 0, i, 0)),
            pl.BlockSpec((1, B_HEADS, TM // BQ, V_ROWS, BQ), lambda b, i: (b, 0, i, 0, 0)),
        ],
        out_shape=[
            jax.ShapeDtypeStruct((B, B_HEADS, 2 * HEAD_DIM, S), BF16),
            jax.ShapeDtypeStruct((B, B_HEADS // 2, S, 2 * HEAD_DIM), BF16),
            jax.ShapeDtypeStruct((B, B_HEADS, S // BQ, V_ROWS, BQ), BF16),
        ],
        compiler_params=_params(("parallel", "parallel")),
        name="proj_b",
    )(xb, w_b)


def _proj_c_kernel(x_ref, w_ref, gq_ref, gkv_ref, wuq_ref, wukv_ref, c1_ref, s1_ref, q_ref, k_ref, v_ref):
    h = _dot(x_ref[...], w_ref[...])
    c1, s1 = c1_ref[...], s1_ref[...]

    def rms(x, g):
        ms = jnp.mean(x * x, axis=-1, keepdims=True)
        return (x * lax.rsqrt(ms + NORM_EPS) * g).astype(BF16)

    cq = rms(h[:, 0:C_Q_LORA], gq_ref[...])
    ckv = rms(h[:, C_Q_LORA:C_Q_LORA + C_KV_LORA], gkv_ref[...])
    qcT = _dot(cq, wuq_ref[...]).T
    kvT = _dot(ckv, wukv_ref[...]).T
    krT = h[:, C_Q_LORA + C_KV_LORA:].T
    kr1, kr2 = _rope_rows(krT[0:16], krT[16:32], c1, s1)
    pad = jnp.zeros((C_QK - C_NOPE - C_ROPE, TM), F32)
    nope_w = C_HEADS * C_NOPE
    for hd in range(C_HEADS):
        r0 = nope_w + hd * C_ROPE
        qr1, qr2 = _rope_rows(qcT[r0:r0 + 16], qcT[r0 + 16:r0 + 32], c1, s1)
        q_ref[0, hd] = jnp.concatenate(
            [qcT[hd * C_NOPE:(hd + 1) * C_NOPE], qr1, qr2, pad], axis=0).astype(BF16)
        kT = jnp.concatenate([kvT[hd * C_NOPE:(hd + 1) * C_NOPE], kr1, kr2, pad], axis=0)
        k_ref[0, hd] = kT.T.astype(BF16)
        v_ref[0, hd, 0] = _value_rows(kvT[nope_w + hd * C_V:nope_w + (hd + 1) * C_V])


def _proj_c(xb, w_c, gq, gkv, w_uq, w_ukv, c1, s1, B, S):
    nt = S // TM
    tab = pl.BlockSpec((ROPE_FREQS, TM), lambda b, i: (0, i))
    full = lambda a: pl.BlockSpec(a.shape, lambda b, i: (0,) * a.ndim)
    gq2, gkv2 = gq.reshape(1, -1), gkv.reshape(1, -1)
    return pl.pallas_call(
        _proj_c_kernel,
        grid=(B, nt),
        in_specs=[
            pl.BlockSpec((TM, D_MODEL), lambda b, i: (b * nt + i, 0)),
            full(w_c), full(gq2), full(gkv2), full(w_uq), full(w_ukv), tab, tab,
        ],
        out_specs=[
            pl.BlockSpec((1, C_HEADS, C_QK, TM), lambda b, i: (b, 0, 0, i)),
            pl.BlockSpec((1, C_HEADS, TM, C_QK), lambda b, i: (b, 0, i, 0)),
            pl.BlockSpec((1, C_HEADS, 1, V_ROWS, TM), lambda b, i: (b, 0, i, 0, 0)),
        ],
        out_shape=[
            jax.ShapeDtypeStruct((B, C_HEADS, C_QK, S), BF16),
            jax.ShapeDtypeStruct((B, C_HEADS, S, C_QK), BF16),
            jax.ShapeDtypeStruct((B, C_HEADS, S // TK, V_ROWS, TK), BF16),
        ],
        compiler_params=_params(("parallel", "parallel")),
        name="proj_c",
    )(xb, w_c, gq2, gkv2, w_uq, w_ukv, c1, s1)


def _dense_attn_kernel(q_ref, k_ref, v_ref, o_ref, m_ref, acc_ref, *, group, tq, n_chunks, scale):
    q = jnp.concatenate([q_ref[0, g] for g in range(group)], axis=1) if group > 1 else q_ref[0, 0]
    m_ref[...] = jnp.full(m_ref.shape, NEG_INF, F32)
    acc_ref[...] = jnp.zeros(acc_ref.shape, F32)

    def chunk(c, carry):
        kc = k_ref[0, 0, pl.ds(pl.multiple_of(c * TK, TK), TK), :]
        s = _dot(kc, q)
        if scale != 1.0:
            s = s * scale
        m_old = m_ref[...]
        m_new = jnp.maximum(m_old, jnp.max(s, axis=0, keepdims=True))
        p = jnp.exp(s - m_new).astype(BF16)
        vc = v_ref[0, 0, c]
        if group > 1:
            pv = jnp.concatenate(
                [_dot(vc, p[:, g * tq:(g + 1) * tq]) for g in range(group)],
                axis=1)
        else:
            pv = _dot(vc, p)
        acc_ref[...] = acc_ref[...] * jnp.exp(m_old - m_new) + pv
        m_ref[...] = m_new
        return carry

    lax.fori_loop(0, n_chunks, chunk, 0)
    acc = acc_ref[...]
    o = acc[0:HEAD_DIM] / acc[HEAD_DIM:HEAD_DIM + 1]
    for g in range(group):
        o_ref[0, g * HEAD_DIM:(g + 1) * HEAD_DIM, :] = o[:, g * tq:(g + 1) * tq].astype(BF16)


def _dense_attn(qT, k, vT, *, group, tq, scale, name):
    B, hq, dk, S = qT.shape
    hkv_arrays = k.shape[1]
    n_kv = hq // group
    kernel = functools.partial(_dense_attn_kernel, group=group, tq=tq, n_chunks=S // TK, scale=scale)
    k_map = (lambda b, h, i: (b, h, 0, 0)) if hkv_arrays > 1 else (lambda b, h, i: (b, 0, 0, 0))
    return pl.pallas_call(
        kernel,
        grid=(B, n_kv, S // tq),
        in_specs=[
            pl.BlockSpec((1, group, dk, tq), lambda b, h, i: (b, h, 0, i)),
            pl.BlockSpec((1, 1, S, dk), k_map),
            pl.BlockSpec((1, 1, S // TK, V_ROWS, TK), lambda b, h, i: (b, h, 0, 0, 0)),
        ],
        out_specs=pl.BlockSpec((1, group * HEAD_DIM, tq), lambda b, h, i: (b, h, i)),
        out_shape=jax.ShapeDtypeStruct((B, hq * HEAD_DIM, S), BF16),
        scratch_shapes=[pltpu.VMEM((1, group * tq), F32), pltpu.VMEM((V_ROWS, group * tq), F32)],
        compiler_params=_params(("parallel", "parallel", "parallel")),
        name=name,
    )(qT, k, vT)


def _alibi_window_table():
    off = (jnp.arange(2 * B_REACH + 1, dtype=jnp.int32) - B_REACH)[:, None, None] * BQ
    kk = jnp.arange(BQ, dtype=jnp.int32)[None, :, None]
    qq = jnp.arange(BQ, dtype=jnp.int32)[None, None, :]
    dist = jnp.abs(off + kk - qq)
    mult = sum(((dist % dilation == 0) & (dist <= window // 2)).astype(F32)
               for window, dilation in zip(B_WINDOWS, B_DILATIONS))
    slopes = jnp.exp2(-ALIBI_MAX_BIAS * jnp.arange(1, B_HEADS + 1, dtype=F32) / B_HEADS)
    bias = -slopes[:, None, None, None] * dist.astype(F32)[None] + jnp.log(jnp.maximum(mult, 1.0))[None]
    return jnp.where(mult[None] > 0, bias, NEG_INF)


def _window_attn_kernel(q_ref, k_ref, v_ref, t_ref, o_ref, m_ref, acc_ref, *, n_chunks):
    qi = pl.program_id(2)
    q = q_ref[0, 0]
    m_ref[...] = jnp.full(m_ref.shape, NEG_INF, F32)
    acc_ref[...] = jnp.zeros(acc_ref.shape, F32)
    order = [B_REACH] + [j for d in range(1, B_REACH + 1) for j in (B_REACH - d, B_REACH + d)]
    for j in order:
        c = qi + (j - B_REACH)

        @pl.when((c >= 0) & (c < n_chunks))
        def _():
            kc = k_ref[0, 0, pl.ds(pl.multiple_of(c * BQ, BQ), BQ), :]
            s = _dot(kc, q) + t_ref[0, j]
            m_old = m_ref[...]
            m_new = jnp.maximum(m_old, jnp.max(s, axis=0, keepdims=True))
            p = jnp.exp(s - m_new).astype(BF16)
            pv = _dot(v_ref[0, 0, c], p)
            acc_ref[...] = acc_ref[...] * jnp.exp(m_old - m_new) + pv
            m_ref[...] = m_new

    acc = acc_ref[...]
    o_ref[0] = (acc[0:HEAD_DIM] / acc[HEAD_DIM:HEAD_DIM + 1]).astype(BF16)


def _window_attn(qT, k, vT, table):
    B, _, _, S = qT.shape
    kernel = functools.partial(_window_attn_kernel, n_chunks=S // BQ)
    return pl.pallas_call(
        kernel,
        grid=(B, B_HEADS, S // BQ),
        in_specs=[
            pl.BlockSpec((1, 1, 2 * HEAD_DIM, BQ), lambda b, h, i: (b, h, 0, i)),
            pl.BlockSpec((1, 1, S, 2 * HEAD_DIM), lambda b, h, i: (b, h // 2, 0, 0)),
            pl.BlockSpec((1, 1, S // BQ, V_ROWS, BQ), lambda b, h, i: (b, h, 0, 0, 0)),
            pl.BlockSpec((1, 2 * B_REACH + 1, BQ, BQ), lambda b, h, i: (h, 0, 0, 0)),
        ],
        out_specs=pl.BlockSpec((1, HEAD_DIM, BQ), lambda b, h, i: (b, h, i)),
        out_shape=jax.ShapeDtypeStruct((B, B_W, S), BF16),
        scratch_shapes=[pltpu.VMEM((1, BQ), F32), pltpu.VMEM((V_ROWS, BQ), F32)],
        compiler_params=_params(("parallel", "parallel", "parallel")),
        name="window_attn",
    )(qT, k, vT, table)


def _out_proj_kernel(a_ref, b_ref, c_ref, w_ref, x_ref, g_ref, beta_ref, of_ref, ob_ref):
    mixT = jnp.concatenate([a_ref[0], b_ref[0], c_ref[0]], axis=0)
    mix = mixT.astype(F32).T.astype(BF16)
    y = _dot(mix, w_ref[...])
    x1 = _layer_norm(ALPHA * x_ref[...] + y, g_ref[...], beta_ref[...])
    of_ref[...] = x1
    ob_ref[...] = x1.astype(BF16)


def _out_proj(aT, bT, cT, w_o, x, g, beta, B, S):
    nt = S // TM
    n = B * S
    row = pl.BlockSpec((TM, D_MODEL), lambda b, i: (b * nt + i, 0))
    vec = pl.BlockSpec((1, D_MODEL), lambda b, i: (0, 0))
    featT = lambda a: pl.BlockSpec((1, a.shape[1], TM), lambda b, i: (b, 0, i))
    return pl.pallas_call(
        _out_proj_kernel,
        grid=(B, nt),
        in_specs=[featT(aT), featT(bT), featT(cT), pl.BlockSpec(w_o.shape, lambda b, i: (0, 0)), row, vec, vec],
        out_specs=[row, row],
        out_shape=[jax.ShapeDtypeStruct((n, D_MODEL), F32), jax.ShapeDtypeStruct((n, D_MODEL), BF16)],
        compiler_params=_params(("parallel", "parallel")),
        name="out_proj",
    )(aT, bT, cT, w_o, x, g.reshape(1, -1), beta.reshape(1, -1))


HALO = 16


def _ffn_kernel(xb_ref, prev_ref, next_ref, xf_ref, wg_ref, wv_ref, wc_ref, bc_ref, wd_ref, g_ref, beta_ref,
                of_ref, ob_ref, acc_ref, *, n_tiles):
    i = pl.program_id(1)
    j = pl.program_id(2)

    @pl.when(j == 0)
    def _():
        acc_ref[...] = jnp.zeros(acc_ref.shape, F32)

    xb = xb_ref[...]
    wg = wg_ref[...]
    gate = _dot(xb, wg)
    val = _dot(xb, wv_ref[...])
    g_prev = _dot(prev_ref[...], wg)[HALO - 1:HALO]
    g_next = _dot(next_ref[...], wg)[0:1]
    g_prev = jnp.where(i > 0, g_prev, 0.0)
    g_next = jnp.where(i < n_tiles - 1, g_next, 0.0)
    row = lax.broadcasted_iota(jnp.int32, gate.shape, 0)
    up = jnp.where(row == 0, g_prev, pltpu.roll(gate, 1, axis=0))
    down = jnp.where(row == TM - 1, g_next, pltpu.roll(gate, TM - 1, axis=0))
    wc = wc_ref[...]
    conv = up * wc[0:1] + gate * wc[1:2] + down * wc[2:3] + bc_ref[...]
    act = 0.5 * conv * (1.0 + lax.erf(conv * (2.0 ** -0.5)))
    hid = (act * val).astype(BF16)
    acc_ref[...] += _dot(hid, wd_ref[...])

    @pl.when(j == pl.num_programs(2) - 1)
    def _():
        x2 = _layer_norm(ALPHA * xf_ref[...] + acc_ref[...], g_ref[...], beta_ref[...])
        of_ref[...] = x2
        ob_ref[...] = x2.astype(BF16)


def _ffn(xb, xf, w_gate, w_val, w_conv, b_conv, w_down, g, beta, B, S):
    nt = S // TM
    n = B * S
    hb = TM // HALO
    last_halo = n // HALO - 1
    row = pl.BlockSpec((TM, D_MODEL), lambda b, i, j: (b * nt + i, 0))
    vec = pl.BlockSpec((1, D_MODEL), lambda b, i, j: (0, 0))
    kernel = functools.partial(_ffn_kernel, n_tiles=nt)
    return pl.pallas_call(
        kernel,
        grid=(B, nt, D_FF // TF),
        in_specs=[
            row,
            pl.BlockSpec((HALO, D_MODEL), lambda b, i, j: (jnp.maximum((b * nt + i) * hb - 1, 0), 0)),
            pl.BlockSpec((HALO, D_MODEL), lambda b, i, j: (jnp.minimum((b * nt + i + 1) * hb, last_halo), 0)),
            row,
            pl.BlockSpec((D_MODEL, TF), lambda b, i, j: (0, j)),
            pl.BlockSpec((D_MODEL, TF), lambda b, i, j: (0, j)),
            pl.BlockSpec((CONV_W, TF), lambda b, i, j: (0, j)),
            pl.BlockSpec((1, TF), lambda b, i, j: (0, j)),
            pl.BlockSpec((TF, D_MODEL), lambda b, i, j: (j, 0)),
            vec, vec,
        ],
        out_specs=[row, row],
        out_shape=[jax.ShapeDtypeStruct((n, D_MODEL), F32), jax.ShapeDtypeStruct((n, D_MODEL), BF16)],
        scratch_shapes=[pltpu.VMEM((TM, D_MODEL), F32)],
        compiler_params=_params(("parallel", "parallel", "arbitrary")),
        name="conv_ffn",
    )(xb, xb, xb, xf, w_gate, w_val, w_conv, b_conv.reshape(1, -1), w_down, g.reshape(1, -1), beta.reshape(1, -1))


def _rope_tables(pos, dim):
    inv_freq = jnp.power(ROPE_THETA, -jnp.arange(0, dim, 2, dtype=F32) / dim)
    ang = pos.astype(F32)[:, None] * inv_freq[None, :]
    return jnp.cos(ang).T, jnp.sin(ang).T


def _position_tables(S):
    rows = S // GRID_W
    row = jnp.repeat(jnp.arange(rows, dtype=jnp.int32), GRID_W)
    col = jnp.tile(jnp.arange(GRID_W, dtype=jnp.int32), rows)
    t = jnp.arange(S, dtype=jnp.int32)
    cr, sr = _rope_tables(row, HEAD_DIM // 2)
    cc, sc = _rope_tables(col, HEAD_DIM // 2)
    c1, s1 = _rope_tables(t, C_ROPE)
    return (cr, sr, cc, sc), (c1, s1)


def _prepare_weights(w_in, w_uq, w_ukv, w_o, w_up, w_down):
    o = np.cumsum([0, A_W, A_KV_W, A_KV_W, B_W, B_W, B_W, C_Q_LORA, C_KV_LORA, C_ROPE])
    layers = []
    uq_cols = np.concatenate(
        [np.concatenate([np.arange(h * 96, h * 96 + C_NOPE) for h in range(C_HEADS)]),
         np.concatenate([np.arange(h * 96 + C_NOPE, (h + 1) * 96) for h in range(C_HEADS)])])
    ukv_cols = np.concatenate(
        [np.concatenate([np.arange(h * 128, h * 128 + C_NOPE) for h in range(C_HEADS)]),
         np.concatenate([np.arange(h * 128 + C_NOPE, (h + 1) * 128) for h in range(C_HEADS)])])
    for l in range(DEPTH):
        wi = w_in[l].astype(BF16)
        w_c = jnp.pad(wi[:, o[6]:o[9]], ((0, 0), (0, 128 - C_ROPE)))
        layers.append(dict(
            w_a=wi[:, o[0]:o[3]], w_b=wi[:, o[3]:o[6]], w_c=w_c,
            w_uq=w_uq[l].astype(BF16)[:, uq_cols], w_ukv=w_ukv[l].astype(BF16)[:, ukv_cols],
            w_o=w_o[l].astype(BF16),
            w_gate=w_up[l, :, :D_FF].astype(BF16), w_val=w_up[l, :, D_FF:].astype(BF16),
            w_down=w_down[l].astype(BF16)))
    return layers


def _trunk(x, weights, table, ln_emb_g, ln_emb_b, a_q_norm, a_k_norm, c_q_norm, c_kv_norm,
           ln1_g, ln1_b, w_conv, b_conv, ln2_g, ln2_b):
    B, S, _ = x.shape
    tabs_a, (c1, s1) = _position_tables(S)
    xf, xb = _embed_ln(x.reshape(B * S, D_MODEL), ln_emb_g, ln_emb_b)
    for l in range(DEPTH):
        w = weights[l]
        aq, ak, av = _proj_a(xb, w["w_a"], a_q_norm[l], a_k_norm[l], tabs_a, B, S)
        bq, bk, bv = _proj_b(xb, w["w_b"], B, S)
        cq, ck, cv = _proj_c(xb, w["w_c"], c_q_norm[l], c_kv_norm[l], w["w_uq"], w["w_ukv"], c1, s1, B, S)
        a_out = _dense_attn(aq, ak.reshape(B, 1, S, A_KV_W), av, group=A_GROUP, tq=TQ_A, scale=1.0, name="attn_a")
        b_out = _window_attn(bq, bk, bv, table)
        c_out = _dense_attn(cq, ck, cv, group=1, tq=TQ_C, scale=(C_NOPE + C_ROPE) ** -0.5, name="attn_c")
        xf, xb = _out_proj(a_out, b_out, c_out, w["w_o"], xf, ln1_g[l], ln1_b[l], B, S)
        xf, xb = _ffn(xb, xf, w["w_gate"], w["w_val"], w_conv[l], b_conv[l], w["w_down"], ln2_g[l], ln2_b[l], B, S)
    return xf.reshape(B, S, D_MODEL)


def kernel(x_prompt, x_sample, ln_emb_g, ln_emb_b, w_in, a_q_norm, a_k_norm, c_q_norm, c_kv_norm, w_uq, w_ukv,
           w_o, ln1_g, ln1_b, w_up, w_conv, b_conv, w_down, ln2_g, ln2_b):
    weights = _prepare_weights(w_in, w_uq, w_ukv, w_o, w_up, w_down)
    table = _alibi_window_table()
    rest = (ln_emb_g, ln_emb_b, a_q_norm, a_k_norm, c_q_norm, c_kv_norm, ln1_g, ln1_b, w_conv, b_conv, ln2_g, ln2_b)
    return (_trunk(x_prompt, weights, table, *rest), _trunk(x_sample, weights, table, *rest))
```

```python
import functools
import math

import numpy as np
import jax
import jax.numpy as jnp
from jax import lax
from jax.experimental import pallas as pl
from jax.experimental.pallas import tpu as pltpu

F32 = jnp.float32
BF16 = jnp.bfloat16

D_MODEL = 2048
DEPTH = 2
HEAD_DIM = 64
GRID_W = 64
ROPE_THETA = 10000.0
NORM_EPS = 1e-6
NEG_INF = -1e30
A_Q_HEADS = 12
A_KV_HEADS = 4
A_GROUP = A_Q_HEADS // A_KV_HEADS
B_HEADS = 12
B_WINDOWS = (128, 512, 2048)
B_DILATIONS = (1, 4, 16)
ALIBI_MAX_BIAS = 8.0
C_HEADS = 8
C_NOPE = 64
C_ROPE = 32
C_V = 64
C_Q_LORA = 512
C_KV_LORA = 256
A_W = A_Q_HEADS * HEAD_DIM
A_KV_W = A_KV_HEADS * HEAD_DIM
B_W = B_HEADS * HEAD_DIM
C_W = C_HEADS * C_V
D_FF = 5632
CONV_W = 3
ALPHA = (2.0 * DEPTH) ** 0.25

ROPE_FREQS = 16
V_ROWS = HEAD_DIM + 16
C_QK = 128

VMEM_LIMIT_BYTES = 56 * 1024 * 1024

TM = 512
TK = 512
TQ_A = 512
TQ_C = 1024
BQ = 256
B_REACH = 4
B_WINDOW = 2 * B_REACH + 1
LOG2E = math.log2(math.e)
TF = 512


def _params(sem):
    return pltpu.CompilerParams(dimension_semantics=sem, vmem_limit_bytes=VMEM_LIMIT_BYTES)


def _dot(a, b):
    return jnp.dot(a, b, preferred_element_type=F32)


def _layer_norm(y, g, b):
    mu = jnp.mean(y, axis=-1, keepdims=True)
    yc = y - mu
    var = jnp.mean(yc * yc, axis=-1, keepdims=True)
    return yc * lax.rsqrt(var + NORM_EPS) * g + b


def _value_rows(vT):
    n = vT.shape[1]
    row = lax.broadcasted_iota(jnp.int32, (V_ROWS - HEAD_DIM, n), 0)
    ones = jnp.where(row == 0, 1.0, 0.0).astype(F32)
    return jnp.concatenate([vT, ones], axis=0).astype(BF16)


def _rope_rows(x1, x2, c, s):
    return x1 * c - x2 * s, x2 * c + x1 * s


def _ln_kernel(x_ref, g_ref, b_ref, of_ref, ob_ref):
    y = _layer_norm(x_ref[...], g_ref[...], b_ref[...])
    of_ref[...] = y
    ob_ref[...] = y.astype(BF16)


def _embed_ln(x, g, b):
    n = x.shape[0]
    row = pl.BlockSpec((TM, D_MODEL), lambda i: (i, 0))
    vec = pl.BlockSpec((1, D_MODEL), lambda i: (0, 0))
    return pl.pallas_call(
        _ln_kernel,
        grid=(n // TM,),
        in_specs=[row, vec, vec],
        out_specs=[row, row],
        out_shape=[jax.ShapeDtypeStruct((n, D_MODEL), F32), jax.ShapeDtypeStruct((n, D_MODEL), BF16)],
        compiler_params=_params(("parallel",)),
        name="embed_ln",
    )(x, g.reshape(1, -1), b.reshape(1, -1))


def _proj_a_kernel(x_ref, w_ref, gq_ref, gk_ref, cr_ref, sr_ref, cc_ref, sc_ref, q_ref, k_ref, v_ref):
    h = _dot(x_ref[...], w_ref[...])
    hT = h.T
    cr, sr, cc, sc = cr_ref[...], sr_ref[...], cc_ref[...], sc_ref[...]

    def norm_rope(blk, g):
        ms = jnp.mean(blk * blk, axis=0, keepdims=True)
        y = blk * lax.rsqrt(ms + NORM_EPS) * g
        r1, r2 = _rope_rows(y[0:16], y[16:32], cr, sr)
        r3, r4 = _rope_rows(y[32:48], y[48:64], cc, sc)
        return jnp.concatenate([r1, r2, r3, r4], axis=0)

    gq, gk = gq_ref[...], gk_ref[...]
    zeros = jnp.zeros((A_KV_W, TM), BF16)
    for hq in range(A_Q_HEADS):
        kvh = hq // A_GROUP
        qh = (norm_rope(hT[hq * HEAD_DIM:(hq + 1) * HEAD_DIM], gq) * (HEAD_DIM ** -0.5 * LOG2E)).astype(BF16)
        q_ref[0, hq] = zeros
        q_ref[0, hq, kvh * HEAD_DIM:(kvh + 1) * HEAD_DIM, :] = qh
    kT = jnp.concatenate(
        [norm_rope(hT[A_W + j * HEAD_DIM:A_W + (j + 1) * HEAD_DIM], gk) for j in range(A_KV_HEADS)], axis=0)
    k_ref[0] = kT.T.astype(BF16)
    v0 = A_W + A_KV_W
    for j in range(A_KV_HEADS):
        v_ref[0, j, 0] = _value_rows(hT[v0 + j * HEAD_DIM:v0 + (j + 1) * HEAD_DIM])


def _proj_a(xb, w_a, gq, gk, tabs, B, S):
    cr, sr, cc, sc = tabs
    nt = S // TM
    tab = pl.BlockSpec((ROPE_FREQS, TM), lambda b, i: (0, i))
    gain = pl.BlockSpec((HEAD_DIM, 1), lambda b, i: (0, 0))
    return pl.pallas_call(
        _proj_a_kernel,
        grid=(B, nt),
        in_specs=[
            pl.BlockSpec((TM, D_MODEL), lambda b, i: (b * nt + i, 0)),
            pl.BlockSpec(w_a.shape, lambda b, i: (0, 0)),
            gain, gain, tab, tab, tab, tab,
        ],
        out_specs=[
            pl.BlockSpec((1, A_Q_HEADS, A_KV_W, TM), lambda b, i: (b, 0, 0, i)),
            pl.BlockSpec((1, TM, A_KV_W), lambda b, i: (b, i, 0)),
            pl.BlockSpec((1, A_KV_HEADS, 1, V_ROWS, TM), lambda b, i: (b, 0, i, 0, 0)),
        ],
        out_shape=[
            jax.ShapeDtypeStruct((B, A_Q_HEADS, A_KV_W, S), BF16),
            jax.ShapeDtypeStruct((B, S, A_KV_W), BF16),
            jax.ShapeDtypeStruct((B, A_KV_HEADS, S // TK, V_ROWS, TK), BF16),
        ],
        compiler_params=_params(("parallel", "parallel")),
        name="proj_a",
    )(xb, w_a, gq.reshape(-1, 1), gk.reshape(-1, 1), cr, sr, cc, sc)


def _proj_b_kernel(x_ref, w_ref, q_ref, k_ref, v_ref):
    h = _dot(x_ref[...], w_ref[...])
    hq = h[:, 0:B_W]
    qT = (hq * (HEAD_DIM ** -0.5 * LOG2E)).T.astype(BF16)
    zeros = jnp.zeros((HEAD_DIM, TM), BF16)
    for hd in range(B_HEADS):
        lo = (hd % 2) * HEAD_DIM
        q_ref[0, hd, lo:lo + HEAD_DIM, :] = qT[hd * HEAD_DIM:(hd + 1) * HEAD_DIM]
        q_ref[0, hd, HEAD_DIM - lo:2 * HEAD_DIM - lo, :] = zeros
    for p in range(B_HEADS // 2):
        k_ref[0, p] = h[:, B_W + p * 128:B_W + (p + 1) * 128].astype(BF16)
    vT = h[:, 2 * B_W:3 * B_W].T
    for hd in range(B_HEADS):
        rows = _value_rows(vT[hd * HEAD_DIM:(hd + 1) * HEAD_DIM])
        for c in range(TM // BQ):
            v_ref[0, hd, c] = rows[:, c * BQ:(c + 1) * BQ]


def _proj_b(xb, w_b, B, S):
    nt = S // TM
    return pl.pallas_call(
        _proj_b_kernel,
        grid=(B, nt),
        in_specs=[
            pl.BlockSpec((TM, D_MODEL), lambda b, i: (b * nt + i, 0)),
            pl.BlockSpec(w_b.shape, lambda b, i: (0, 0)),
        ],
        out_specs=[
            pl.BlockSpec((1, B_HEADS, 2 * HEAD_DIM, TM), lambda b, i: (b, 0, 0, i)),
            pl.BlockSpec((1, B_HEADS // 2, TM, 2 * HEAD_DIM), lambda b, i: (b, 0, i, 0)),
            pl.BlockSpec((1, B_HEADS, TM // BQ, V_ROWS, BQ), lambda b, i: (b, 0, i, 0, 0)),
        ],
        out_shape=[
            jax.ShapeDtypeStruct((B, B_HEADS, 2 * HEAD_DIM, S), BF16),
            jax.ShapeDtypeStruct((B, B_HEADS // 2, S, 2 * HEAD_DIM), BF16),
            jax.ShapeDtypeStruct((B, B_HEADS, S // BQ, V_ROWS, BQ), BF16),
        ],
        compiler_params=_params(("parallel", "parallel")),
        name="proj_b",
    )(xb, w_b)


def _proj_c_kernel(x_ref, w_ref, gq_ref, gkv_ref, wuq_ref, wukv_ref, c1_ref, s1_ref, q_ref, k_ref, v_ref):
    h = _dot(x_ref[...], w_ref[...])
    c1, s1 = c1_ref[...], s1_ref[...]

    def rms(x, g):
        ms = jnp.mean(x * x, axis=-1, keepdims=True)
        return (x * lax.rsqrt(ms + NORM_EPS) * g).astype(BF16)

    cq = rms(h[:, 0:C_Q_LORA], gq_ref[...])
    ckv = rms(h[:, C_Q_LORA:C_Q_LORA + C_KV_LORA], gkv_ref[...])
    qcT = _dot(cq, wuq_ref[...]).T * ((C_NOPE + C_ROPE) ** -0.5 * LOG2E)
    kvT = _dot(ckv, wukv_ref[...]).T
    krT = h[:, C_Q_LORA + C_KV_LORA:].T
    kr1, kr2 = _rope_rows(krT[0:16], krT[16:32], c1, s1)
    pad = jnp.zeros((C_QK - C_NOPE - C_ROPE, TM), F32)
    nope_w = C_HEADS * C_NOPE
    for hd in range(C_HEADS):
        r0 = nope_w + hd * C_ROPE
        qr1, qr2 = _rope_rows(qcT[r0:r0 + 16], qcT[r0 + 16:r0 + 32], c1, s1)
        q_ref[0, hd] = jnp.concatenate(
            [qcT[hd * C_NOPE:(hd + 1) * C_NOPE], qr1, qr2, pad], axis=0).astype(BF16)
        kT = jnp.concatenate([kvT[hd * C_NOPE:(hd + 1) * C_NOPE], kr1, kr2, pad], axis=0)
        k_ref[0, hd] = kT.T.astype(BF16)
        v_ref[0, hd, 0] = _value_rows(kvT[nope_w + hd * C_V:nope_w + (hd + 1) * C_V])


def _proj_c(xb, w_c, gq, gkv, w_uq, w_ukv, c1, s1, B, S):
    nt = S // TM
    tab = pl.BlockSpec((ROPE_FREQS, TM), lambda b, i: (0, i))
    full = lambda a: pl.BlockSpec(a.shape, lambda b, i: (0,) * a.ndim)
    gq2, gkv2 = gq.reshape(1, -1), gkv.reshape(1, -1)
    return pl.pallas_call(
        _proj_c_kernel,
        grid=(B, nt),
        in_specs=[
            pl.BlockSpec((TM, D_MODEL), lambda b, i: (b * nt + i, 0)),
            full(w_c), full(gq2), full(gkv2), full(w_uq), full(w_ukv), tab, tab,
        ],
        out_specs=[
            pl.BlockSpec((1, C_HEADS, C_QK, TM), lambda b, i: (b, 0, 0, i)),
            pl.BlockSpec((1, C_HEADS, TM, C_QK), lambda b, i: (b, 0, i, 0)),
            pl.BlockSpec((1, C_HEADS, 1, V_ROWS, TM), lambda b, i: (b, 0, i, 0, 0)),
        ],
        out_shape=[
            jax.ShapeDtypeStruct((B, C_HEADS, C_QK, S), BF16),
            jax.ShapeDtypeStruct((B, C_HEADS, S, C_QK), BF16),
            jax.ShapeDtypeStruct((B, C_HEADS, S // TK, V_ROWS, TK), BF16),
        ],
        compiler_params=_params(("parallel", "parallel")),
        name="proj_c",
    )(xb, w_c, gq2, gkv2, w_uq, w_ukv, c1, s1)


def _dense_attn_kernel(q_ref, k_ref, v_ref, o_ref, s_ref, smax_ref, m_ref, acc_ref, *, group, tq, n_chunks):
    q = jnp.concatenate([q_ref[0, g] for g in range(group)], axis=1) if group > 1 else q_ref[0, 0]
    m_ref[...] = jnp.full(m_ref.shape, NEG_INF, F32)
    acc_ref[...] = jnp.zeros(acc_ref.shape, F32)

    def fill(slot, c):
        kc = k_ref[0, 0, pl.ds(pl.multiple_of(c * TK, TK), TK), :]
        s = _dot(kc, q)
        s_ref[slot] = s
        smax_ref[slot] = jnp.max(s, axis=0, keepdims=True)

    def consume(slot, c):
        m_old = m_ref[...]
        m_new = jnp.maximum(m_old, smax_ref[slot])
        p = jnp.exp2(s_ref[slot] - m_new).astype(BF16)
        vc = v_ref[0, 0, c]
        if group > 1:
            pv = jnp.concatenate([_dot(vc, p[:, g * tq:(g + 1) * tq]) for g in range(group)], axis=1)
        else:
            pv = _dot(vc, p)
        acc_ref[...] = acc_ref[...] * jnp.exp2(m_old - m_new) + pv
        m_ref[...] = m_new

    fill(0, 0)

    def pair(i, carry):
        c = 2 * i
        fill(1, c + 1)
        consume(0, c)
        fill(0, c + 2)
        consume(1, c + 1)
        return carry

    lax.fori_loop(0, n_chunks // 2 - 1, pair, 0)
    fill(1, n_chunks - 1)
    consume(0, n_chunks - 2)
    consume(1, n_chunks - 1)
    acc = acc_ref[...]
    o = acc[0:HEAD_DIM] / acc[HEAD_DIM:HEAD_DIM + 1]
    for g in range(group):
        o_ref[0, g * HEAD_DIM:(g + 1) * HEAD_DIM, :] = o[:, g * tq:(g + 1) * tq].astype(BF16)


def _dense_attn(qT, k, vT, *, group, tq, name):
    B, hq, dk, S = qT.shape
    hkv_arrays = k.shape[1]
    n_kv = hq // group
    n_chunks = S // TK
    assert n_chunks % 2 == 0 and n_chunks >= 2
    kernel = functools.partial(_dense_attn_kernel, group=group, tq=tq, n_chunks=n_chunks)
    k_map = (lambda b, h, i: (b, h, 0, 0)) if hkv_arrays > 1 else (lambda b, h, i: (b, 0, 0, 0))
    return pl.pallas_call(
        kernel,
        grid=(B, n_kv, S // tq),
        in_specs=[
            pl.BlockSpec((1, group, dk, tq), lambda b, h, i: (b, h, 0, i)),
            pl.BlockSpec((1, 1, S, dk), k_map),
            pl.BlockSpec((1, 1, S // TK, V_ROWS, TK), lambda b, h, i: (b, h, 0, 0, 0)),
        ],
        out_specs=pl.BlockSpec((1, group * HEAD_DIM, tq), lambda b, h, i: (b, h, i)),
        out_shape=jax.ShapeDtypeStruct((B, hq * HEAD_DIM, S), BF16),
        scratch_shapes=[pltpu.VMEM((2, TK, group * tq), F32), pltpu.VMEM((2, 1, group * tq), F32),
                        pltpu.VMEM((1, group * tq), F32),
                        pltpu.VMEM((V_ROWS, group * tq), F32)],
        compiler_params=_params(("parallel", "parallel", "parallel")),
        name=name,
    )(qT, k, vT)


def _alibi_window_table():
    n_off = 4 * B_REACH + 1
    off = (jnp.arange(n_off, dtype=jnp.int32) - 2 * B_REACH)[:, None, None] * BQ
    kk = jnp.arange(BQ, dtype=jnp.int32)[None, :, None]
    qq = jnp.arange(BQ, dtype=jnp.int32)[None, None, :]
    dist = jnp.abs(off + kk - qq)
    mult = sum(((dist % dilation == 0) & (dist <= window // 2)).astype(F32)
               for window, dilation in zip(B_WINDOWS, B_DILATIONS))
    slopes = jnp.exp2(-ALIBI_MAX_BIAS * jnp.arange(1, B_HEADS + 1, dtype=F32) / B_HEADS)
    bias = -slopes[:, None, None, None] * dist.astype(F32)[None] + jnp.log(jnp.maximum(mult, 1.0))[None]
    tbl = jnp.where(mult[None] > 0, bias * LOG2E, NEG_INF)
    return tbl.reshape(B_HEADS, n_off * BQ, BQ)


def _window_attn_kernel(q_ref, k_ref, v_ref, t_ref, o_ref, s_ref, smax_ref, *, n_chunks):
    qi = pl.program_id(2)
    w0 = jnp.clip(qi - B_REACH, 0, n_chunks - B_WINDOW)
    kw = k_ref[0, 0, pl.ds(pl.multiple_of(w0 * BQ, BQ), B_WINDOW * BQ), :]
    t0 = pl.multiple_of((w0 - qi + 2 * B_REACH) * BQ, BQ)

    def fill(hd):
        s = _dot(kw, q_ref[0, hd]) + t_ref[hd, pl.ds(t0, B_WINDOW * BQ), :]
        s_ref[hd] = s
        smax_ref[hd] = jnp.max(s, axis=0, keepdims=True)

    def consume(hd):
        p = jnp.exp2(s_ref[hd] - smax_ref[hd]).astype(BF16)
        acc = _dot(v_ref[0, hd, w0], p[0:BQ])
        for u in range(1, B_WINDOW):
            acc = acc + _dot(v_ref[0, hd, w0 + u], p[u * BQ:(u + 1) * BQ])
        o_ref[0, hd * HEAD_DIM:(hd + 1) * HEAD_DIM, :] = (
            acc[0:HEAD_DIM] / acc[HEAD_DIM:HEAD_DIM + 1]).astype(BF16)

    fill(0)
    fill(1)
    consume(0)
    consume(1)


def _window_attn(qT, k, vT, table):
    B, _, _, S = qT.shape
    n_chunks = S // BQ
    assert n_chunks >= B_WINDOW
    kernel = functools.partial(_window_attn_kernel, n_chunks=n_chunks)
    return pl.pallas_call(
        kernel,
        grid=(B, B_HEADS // 2, n_chunks),
        in_specs=[
            pl.BlockSpec((1, 2, 2 * HEAD_DIM, BQ), lambda b, h, i: (b, h, 0, i)),
            pl.BlockSpec((1, 1, S, 2 * HEAD_DIM), lambda b, h, i: (b, h, 0, 0)),
            pl.BlockSpec((1, 2, n_chunks, V_ROWS, BQ), lambda b, h, i: (b, h, 0, 0, 0)),
            pl.BlockSpec((2, (4 * B_REACH + 1) * BQ, BQ), lambda b, h, i: (h, 0, 0)),
        ],
        out_specs=pl.BlockSpec((1, 2 * HEAD_DIM, BQ), lambda b, h, i: (b, h, i)),
        out_shape=jax.ShapeDtypeStruct((B, B_W, S), BF16),
        scratch_shapes=[pltpu.VMEM((2, B_WINDOW * BQ, BQ), F32), pltpu.VMEM((2, 1, BQ), F32)],
        compiler_params=_params(("parallel", "parallel", "parallel")),
        name="window_attn",
    )(qT, k, vT, table)


def _out_proj_kernel(a_ref, b_ref, c_ref, w_ref, x_ref, g_ref, beta_ref, of_ref, ob_ref):
    mixT = jnp.concatenate([a_ref[0], b_ref[0], c_ref[0]], axis=0)
    mix = mixT.astype(F32).T.astype(BF16)
    y = _dot(mix, w_ref[...])
    x1 = _layer_norm(ALPHA * x_ref[...] + y, g_ref[...], beta_ref[...])
    of_ref[...] = x1
    ob_ref[...] = x1.astype(BF16)


def _out_proj(aT, bT, cT, w_o, x, g, beta, B, S):
    nt = S // TM
    n = B * S
    row = pl.BlockSpec((TM, D_MODEL), lambda b, i: (b * nt + i, 0))
    vec = pl.BlockSpec((1, D_MODEL), lambda b, i: (0, 0))
    featT = lambda a: pl.BlockSpec((1, a.shape[1], TM), lambda b, i: (b, 0, i))
    return pl.pallas_call(
        _out_proj_kernel,
        grid=(B, nt),
        in_specs=[featT(aT), featT(bT), featT(cT), pl.BlockSpec(w_o.shape, lambda b, i: (0, 0)), row, vec, vec],
        out_specs=[row, row],
        out_shape=[jax.ShapeDtypeStruct((n, D_MODEL), F32), jax.ShapeDtypeStruct((n, D_MODEL), BF16)],
        compiler_params=_params(("parallel", "parallel")),
        name="out_proj",
    )(aT, bT, cT, w_o, x, g.reshape(1, -1), beta.reshape(1, -1))


HALO = 16


def _ffn_kernel(xb_ref, prev_ref, next_ref, xf_ref, wg_ref, wv_ref, wc_ref, bc_ref, wd_ref, g_ref, beta_ref,
                of_ref, ob_ref, acc_ref, *, n_tiles):
    i = pl.program_id(1)
    j = pl.program_id(2)

    @pl.when(j == 0)
    def _():
        acc_ref[...] = jnp.zeros(acc_ref.shape, F32)

    xb = xb_ref[...]
    wg = wg_ref[...]
    gate = _dot(xb, wg)
    val = _dot(xb, wv_ref[...])
    g_prev = _dot(prev_ref[...], wg)[HALO - 1:HALO]
    g_next = _dot(next_ref[...], wg)[0:1]
    g_prev = jnp.where(i > 0, g_prev, 0.0)
    g_next = jnp.where(i < n_tiles - 1, g_next, 0.0)
    row = lax.broadcasted_iota(jnp.int32, gate.shape, 0)
    up = jnp.where(row == 0, g_prev, pltpu.roll(gate, 1, axis=0))
    down = jnp.where(row == TM - 1, g_next, pltpu.roll(gate, TM - 1, axis=0))
    wc = wc_ref[...]
    conv = up * wc[0:1] + gate * wc[1:2] + down * wc[2:3] + bc_ref[...]
    act = 0.5 * conv * (1.0 + lax.erf(conv * (2.0 ** -0.5)))
    hid = (act * val).astype(BF16)
    acc_ref[...] += _dot(hid, wd_ref[...])

    @pl.when(j == pl.num_programs(2) - 1)
    def _():
        x2 = _layer_norm(ALPHA * xf_ref[...] + acc_ref[...], g_ref[...], beta_ref[...])
        of_ref[...] = x2
        ob_ref[...] = x2.astype(BF16)


def _ffn(xb, xf, w_gate, w_val, w_conv, b_conv, w_down, g, beta, B, S):
    nt = S // TM
    n = B * S
    hb = TM // HALO
    last_halo = n // HALO - 1
    row = pl.BlockSpec((TM, D_MODEL), lambda b, i, j: (b * nt + i, 0))
    vec = pl.BlockSpec((1, D_MODEL), lambda b, i, j: (0, 0))
    kernel = functools.partial(_ffn_kernel, n_tiles=nt)
    return pl.pallas_call(
        kernel,
        grid=(B, nt, D_FF // TF),
        in_specs=[
            row,
            pl.BlockSpec((HALO, D_MODEL), lambda b, i, j: (jnp.maximum((b * nt + i) * hb - 1, 0), 0)),
            pl.BlockSpec((HALO, D_MODEL), lambda b, i, j: (jnp.minimum((b * nt + i + 1) * hb, last_halo), 0)),
            row,
            pl.BlockSpec((D_MODEL, TF), lambda b, i, j: (0, j)),
            pl.BlockSpec((D_MODEL, TF), lambda b, i, j: (0, j)),
            pl.BlockSpec((CONV_W, TF), lambda b, i, j: (0, j)),
            pl.BlockSpec((1, TF), lambda b, i, j: (0, j)),
            pl.BlockSpec((TF, D_MODEL), lambda b, i, j: (j, 0)),
            vec, vec,
        ],
        out_specs=[row, row],
        out_shape=[jax.ShapeDtypeStruct((n, D_MODEL), F32), jax.ShapeDtypeStruct((n, D_MODEL), BF16)],
        scratch_shapes=[pltpu.VMEM((TM, D_MODEL), F32)],
        compiler_params=_params(("parallel", "parallel", "arbitrary")),
        name="conv_ffn",
    )(xb, xb, xb, xf, w_gate, w_val, w_conv, b_conv.reshape(1, -1), w_down, g.reshape(1, -1), beta.reshape(1, -1))


def _rope_tables(pos, dim):
    inv_freq = jnp.power(ROPE_THETA, -jnp.arange(0, dim, 2, dtype=F32) / dim)
    ang = pos.astype(F32)[:, None] * inv_freq[None, :]
    return jnp.cos(ang).T, jnp.sin(ang).T


def _position_tables(S):
    rows = S // GRID_W
    row = jnp.repeat(jnp.arange(rows, dtype=jnp.int32), GRID_W)
    col = jnp.tile(jnp.arange(GRID_W, dtype=jnp.int32), rows)
    t = jnp.arange(S, dtype=jnp.int32)
    cr, sr = _rope_tables(row, HEAD_DIM // 2)
    cc, sc = _rope_tables(col, HEAD_DIM // 2)
    c1, s1 = _rope_tables(t, C_ROPE)
    return (cr, sr, cc, sc), (c1, s1)


def _prepare_weights(w_in, w_uq, w_ukv, w_o, w_up, w_down):
    o = np.cumsum([0, A_W, A_KV_W, A_KV_W, B_W, B_W, B_W, C_Q_LORA, C_KV_LORA, C_ROPE])
    layers = []
    uq_cols = np.concatenate(
        [np.concatenate([np.arange(h * 96, h * 96 + C_NOPE) for h in range(C_HEADS)]),
         np.concatenate([np.arange(h * 96 + C_NOPE, (h + 1) * 96) for h in range(C_HEADS)])])
    ukv_cols = np.concatenate(
        [np.concatenate([np.arange(h * 128, h * 128 + C_NOPE) for h in range(C_HEADS)]),
         np.concatenate([np.arange(h * 128 + C_NOPE, (h + 1) * 128) for h in range(C_HEADS)])])
    for l in range(DEPTH):
        wi = w_in[l].astype(BF16)
        w_c = jnp.pad(wi[:, o[6]:o[9]], ((0, 0), (0, 128 - C_ROPE)))
        layers.append(dict(
            w_a=wi[:, o[0]:o[3]], w_b=wi[:, o[3]:o[6]], w_c=w_c,
            w_uq=w_uq[l].astype(BF16)[:, uq_cols], w_ukv=w_ukv[l].astype(BF16)[:, ukv_cols],
            w_o=w_o[l].astype(BF16),
            w_gate=w_up[l, :, :D_FF].astype(BF16), w_val=w_up[l, :, D_FF:].astype(BF16),
            w_down=w_down[l].astype(BF16)))
    return layers


def _trunk(x, weights, table, ln_emb_g, ln_emb_b, a_q_norm, a_k_norm, c_q_norm, c_kv_norm,
           ln1_g, ln1_b, w_conv, b_conv, ln2_g, ln2_b):
    B, S, _ = x.shape
    tabs_a, (c1, s1) = _position_tables(S)
    xf, xb = _embed_ln(x.reshape(B * S, D_MODEL), ln_emb_g, ln_emb_b)
    for l in range(DEPTH):
        w = weights[l]
        aq, ak, av = _proj_a(xb, w["w_a"], a_q_norm[l], a_k_norm[l], tabs_a, B, S)
        bq, bk, bv = _proj_b(xb, w["w_b"], B, S)
        cq, ck, cv = _proj_c(xb, w["w_c"], c_q_norm[l], c_kv_norm[l], w["w_uq"], w["w_ukv"], c1, s1, B, S)
        a_out = _dense_attn(aq, ak.reshape(B, 1, S, A_KV_W), av, group=A_GROUP, tq=TQ_A, name="attn_a")
        b_out = _window_attn(bq, bk, bv, table)
        c_out = _dense_attn(cq, ck, cv, group=1, tq=TQ_C, name="attn_c")
        xf, xb = _out_proj(a_out, b_out, c_out, w["w_o"], xf, ln1_g[l], ln1_b[l], B, S)
        xf, xb = _ffn(xb, xf, w["w_gate"], w["w_val"], w_conv[l], b_conv[l], w["w_down"], ln2_g[l], ln2_b[l], B, S)
    return xf.reshape(B, S, D_MODEL)


def kernel(x_prompt, x_sample, ln_emb_g, ln_emb_b, w_in, a_q_norm, a_k_norm, c_q_norm, c_kv_norm, w_uq, w_ukv,
           w_o, ln1_g, ln1_b, w_up, w_conv, b_conv, w_down, ln2_g, ln2_b):
    weights = _prepare_weights(w_in, w_uq, w_ukv, w_o, w_up, w_down)
    table = _alibi_window_table()
    rest = (ln_emb_g, ln_emb_b, a_q_norm, a_k_norm, c_q_norm, c_kv_norm, ln1_g, ln1_b, w_conv, b_conv, ln2_g, ln2_b)
    return (_trunk(x_prompt, weights, table, *rest), _trunk(x_sample, weights, table, *rest))
```

```python
import functools
import math

import numpy as np
import jax
import jax.numpy as jnp
from jax import lax
from jax.experimental import pallas as pl
from jax.experimental.pallas import tpu as pltpu

F32 = jnp.float32
BF16 = jnp.bfloat16

D_MODEL = 2048
DEPTH = 2
HEAD_DIM = 64
GRID_W = 64
ROPE_THETA = 10000.0
NORM_EPS = 1e-6
NEG_INF = -1e30
A_Q_HEADS = 12
A_KV_HEADS = 4
A_GROUP = A_Q_HEADS // A_KV_HEADS
B_HEADS = 12
B_WINDOWS = (128, 512, 2048)
B_DILATIONS = (1, 4, 16)
ALIBI_MAX_BIAS = 8.0
C_HEADS = 8
C_NOPE = 64
C_ROPE = 32
C_V = 64
C_Q_LORA = 512
C_KV_LORA = 256
A_W = A_Q_HEADS * HEAD_DIM
A_KV_W = A_KV_HEADS * HEAD_DIM
B_W = B_HEADS * HEAD_DIM
C_W = C_HEADS * C_V
D_FF = 5632
CONV_W = 3
ALPHA = (2.0 * DEPTH) ** 0.25

ROPE_FREQS = 16
V_ROWS = HEAD_DIM + 16
C_QK = 128

VMEM_LIMIT_BYTES = 56 * 1024 * 1024

TM = 512
TK_A = 512
TK_C = 512
CHUNKS_PER_TRIP_A = 4
CHUNKS_PER_TRIP_C = 8
TQ_A = 512
TQ_C = 1024
BQ = 256
B_REACH = 4
B_WINDOW = 2 * B_REACH + 1
B_QCHUNKS = 4
LOG2E = math.log2(math.e)
TF = 512
TM_UP = 1024
TM_DOWN = 256


def _params(sem):
    return pltpu.CompilerParams(dimension_semantics=sem, vmem_limit_bytes=VMEM_LIMIT_BYTES)


def _dot(a, b):
    return jnp.dot(a, b, preferred_element_type=F32)


def _layer_norm(y, g, b):
    mu = jnp.mean(y, axis=-1, keepdims=True)
    yc = y - mu
    var = jnp.mean(yc * yc, axis=-1, keepdims=True)
    return yc * lax.rsqrt(var + NORM_EPS) * g + b


def _value_rows(vT):
    n = vT.shape[1]
    row = lax.broadcasted_iota(jnp.int32, (V_ROWS - HEAD_DIM, n), 0)
    ones = jnp.where(row == 0, 1.0, 0.0).astype(F32)
    return jnp.concatenate([vT, ones], axis=0).astype(BF16)


def _rope_rows(x1, x2, c, s):
    return x1 * c - x2 * s, x2 * c + x1 * s


def _ln_kernel(x_ref, g_ref, b_ref, of_ref, ob_ref):
    y = _layer_norm(x_ref[...], g_ref[...], b_ref[...])
    of_ref[...] = y
    ob_ref[...] = y.astype(BF16)


def _embed_ln(x, g, b):
    n = x.shape[0]
    row = pl.BlockSpec((TM, D_MODEL), lambda i: (i, 0))
    vec = pl.BlockSpec((1, D_MODEL), lambda i: (0, 0))
    return pl.pallas_call(
        _ln_kernel,
        grid=(n // TM,),
        in_specs=[row, vec, vec],
        out_specs=[row, row],
        out_shape=[jax.ShapeDtypeStruct((n, D_MODEL), F32), jax.ShapeDtypeStruct((n, D_MODEL), BF16)],
        compiler_params=_params(("parallel",)),
        name="embed_ln",
    )(x, g.reshape(1, -1), b.reshape(1, -1))


def _proj_a_kernel(x_ref, w_ref, gq_ref, gk_ref, cr_ref, sr_ref, cc_ref, sc_ref, q_ref, k_ref, v_ref):
    h = _dot(x_ref[...], w_ref[...])
    hT = h.T
    cr, sr, cc, sc = cr_ref[...], sr_ref[...], cc_ref[...], sc_ref[...]

    def norm_rope(blk, g):
        ms = jnp.mean(blk * blk, axis=0, keepdims=True)
        y = blk * lax.rsqrt(ms + NORM_EPS) * g
        r1, r2 = _rope_rows(y[0:16], y[16:32], cr, sr)
        r3, r4 = _rope_rows(y[32:48], y[48:64], cc, sc)
        return jnp.concatenate([r1, r2, r3, r4], axis=0)

    gq, gk = gq_ref[...], gk_ref[...]
    zeros = jnp.zeros((A_KV_W, TM), BF16)
    for hq in range(A_Q_HEADS):
        kvh = hq // A_GROUP
        qh = (norm_rope(hT[hq * HEAD_DIM:(hq + 1) * HEAD_DIM], gq) * (HEAD_DIM ** -0.5 * LOG2E)).astype(BF16)
        q_ref[0, hq] = zeros
        q_ref[0, hq, kvh * HEAD_DIM:(kvh + 1) * HEAD_DIM, :] = qh
    kT = jnp.concatenate(
        [norm_rope(hT[A_W + j * HEAD_DIM:A_W + (j + 1) * HEAD_DIM], gk) for j in range(A_KV_HEADS)], axis=0)
    k_ref[0] = kT.T.astype(BF16)
    v0 = A_W + A_KV_W
    for j in range(A_KV_HEADS):
        rows = _value_rows(hT[v0 + j * HEAD_DIM:v0 + (j + 1) * HEAD_DIM])
        for c in range(TM // TK_A):
            v_ref[0, j, c] = rows[:, c * TK_A:(c + 1) * TK_A]


def _proj_a(xb, w_a, gq, gk, tabs, B, S):
    cr, sr, cc, sc = tabs
    nt = S // TM
    tab = pl.BlockSpec((ROPE_FREQS, TM), lambda b, i: (0, i))
    gain = pl.BlockSpec((HEAD_DIM, 1), lambda b, i: (0, 0))
    return pl.pallas_call(
        _proj_a_kernel,
        grid=(B, nt),
        in_specs=[
            pl.BlockSpec((TM, D_MODEL), lambda b, i: (b * nt + i, 0)),
            pl.BlockSpec(w_a.shape, lambda b, i: (0, 0)),
            gain, gain, tab, tab, tab, tab,
        ],
        out_specs=[
            pl.BlockSpec((1, A_Q_HEADS, A_KV_W, TM), lambda b, i: (b, 0, 0, i)),
            pl.BlockSpec((1, TM, A_KV_W), lambda b, i: (b, i, 0)),
            pl.BlockSpec((1, A_KV_HEADS, TM // TK_A, V_ROWS, TK_A), lambda b, i: (b, 0, i, 0, 0)),
        ],
        out_shape=[
            jax.ShapeDtypeStruct((B, A_Q_HEADS, A_KV_W, S), BF16),
            jax.ShapeDtypeStruct((B, S, A_KV_W), BF16),
            jax.ShapeDtypeStruct((B, A_KV_HEADS, S // TK_A, V_ROWS, TK_A), BF16),
        ],
        compiler_params=_params(("parallel", "parallel")),
        name="proj_a",
    )(xb, w_a, gq.reshape(-1, 1), gk.reshape(-1, 1), cr, sr, cc, sc)


def _proj_b_kernel(x_ref, w_ref, q_ref, k_ref, v_ref):
    h = _dot(x_ref[...], w_ref[...])
    hq = h[:, 0:B_W]
    qT = (hq * (HEAD_DIM ** -0.5 * LOG2E)).T.astype(BF16)
    zeros = jnp.zeros((HEAD_DIM, TM), BF16)
    for hd in range(B_HEADS):
        lo = (hd % 2) * HEAD_DIM
        q_ref[0, hd, lo:lo + HEAD_DIM, :] = qT[hd * HEAD_DIM:(hd + 1) * HEAD_DIM]
        q_ref[0, hd, HEAD_DIM - lo:2 * HEAD_DIM - lo, :] = zeros
    for p in range(B_HEADS // 2):
        k_ref[0, p] = h[:, B_W + p * 128:B_W + (p + 1) * 128].astype(BF16)
    vT = h[:, 2 * B_W:3 * B_W].T
    for hd in range(B_HEADS):
        rows = _value_rows(vT[hd * HEAD_DIM:(hd + 1) * HEAD_DIM])
        for c in range(TM // BQ):
            v_ref[0, hd, c] = rows[:, c * BQ:(c + 1) * BQ]


def _proj_b(xb, w_b, B, S):
    nt = S // TM
    return pl.pallas_call(
        _proj_b_kernel,
        grid=(B, nt),
        in_specs=[
            pl.BlockSpec((TM, D_MODEL), lambda b, i: (b * nt + i, 0)),
            pl.BlockSpec(w_b.shape, lambda b, i: (0, 0)),
        ],
        out_specs=[
            pl.BlockSpec((1, B_HEADS, 2 * HEAD_DIM, TM), lambda b, i: (b, 0, 0, i)),
            pl.BlockSpec((1, B_HEADS // 2, TM, 2 * HEAD_DIM), lambda b, i: (b, 0, i, 0)),
            pl.BlockSpec((1, B_HEADS, TM // BQ, V_ROWS, BQ), lambda b, i: (b, 0, i, 0, 0)),
        ],
        out_shape=[
            jax.ShapeDtypeStruct((B, B_HEADS, 2 * HEAD_DIM, S), BF16),
            jax.ShapeDtypeStruct((B, B_HEADS // 2, S, 2 * HEAD_DIM), BF16),
            jax.ShapeDtypeStruct((B, B_HEADS, S // BQ, V_ROWS, BQ), BF16),
        ],
        compiler_params=_params(("parallel", "parallel")),
        name="proj_b",
    )(xb, w_b)


def _proj_c_kernel(x_ref, w_ref, gq_ref, gkv_ref, wuq_ref, wukv_ref, c1_ref, s1_ref, q_ref, k_ref, v_ref):
    h = _dot(x_ref[...], w_ref[...])
    c1, s1 = c1_ref[...], s1_ref[...]

    def rms(x, g):
        ms = jnp.mean(x * x, axis=-1, keepdims=True)
        return (x * lax.rsqrt(ms + NORM_EPS) * g).astype(BF16)

    cq = rms(h[:, 0:C_Q_LORA], gq_ref[...])
    ckv = rms(h[:, C_Q_LORA:C_Q_LORA + C_KV_LORA], gkv_ref[...])
    qcT = _dot(cq, wuq_ref[...]).T * ((C_NOPE + C_ROPE) ** -0.5 * LOG2E)
    kvT = _dot(ckv, wukv_ref[...]).T
    krT = h[:, C_Q_LORA + C_KV_LORA:].T
    kr1, kr2 = _rope_rows(krT[0:16], krT[16:32], c1, s1)
    pad = jnp.zeros((C_QK - C_NOPE - C_ROPE, TM), F32)
    nope_w = C_HEADS * C_NOPE
    for hd in range(C_HEADS):
        r0 = nope_w + hd * C_ROPE
        qr1, qr2 = _rope_rows(qcT[r0:r0 + 16], qcT[r0 + 16:r0 + 32], c1, s1)
        q_ref[0, hd] = jnp.concatenate(
            [qcT[hd * C_NOPE:(hd + 1) * C_NOPE], qr1, qr2, pad], axis=0).astype(BF16)
        kT = jnp.concatenate([kvT[hd * C_NOPE:(hd + 1) * C_NOPE], kr1, kr2, pad], axis=0)
        k_ref[0, hd] = kT.T.astype(BF16)
        rows = _value_rows(kvT[nope_w + hd * C_V:nope_w + (hd + 1) * C_V])
        for c in range(TM // TK_C):
            v_ref[0, hd, c] = rows[:, c * TK_C:(c + 1) * TK_C]


def _proj_c(xb, w_c, gq, gkv, w_uq, w_ukv, c1, s1, B, S):
    nt = S // TM
    tab = pl.BlockSpec((ROPE_FREQS, TM), lambda b, i: (0, i))
    full = lambda a: pl.BlockSpec(a.shape, lambda b, i: (0,) * a.ndim)
    gq2, gkv2 = gq.reshape(1, -1), gkv.reshape(1, -1)
    return pl.pallas_call(
        _proj_c_kernel,
        grid=(B, nt),
        in_specs=[
            pl.BlockSpec((TM, D_MODEL), lambda b, i: (b * nt + i, 0)),
            full(w_c), full(gq2), full(gkv2), full(w_uq), full(w_ukv), tab, tab,
        ],
        out_specs=[
            pl.BlockSpec((1, C_HEADS, C_QK, TM), lambda b, i: (b, 0, 0, i)),
            pl.BlockSpec((1, C_HEADS, TM, C_QK), lambda b, i: (b, 0, i, 0)),
            pl.BlockSpec((1, C_HEADS, TM // TK_C, V_ROWS, TK_C), lambda b, i: (b, 0, i, 0, 0)),
        ],
        out_shape=[
            jax.ShapeDtypeStruct((B, C_HEADS, C_QK, S), BF16),
            jax.ShapeDtypeStruct((B, C_HEADS, S, C_QK), BF16),
            jax.ShapeDtypeStruct((B, C_HEADS, S // TK_C, V_ROWS, TK_C), BF16),
        ],
        compiler_params=_params(("parallel", "parallel")),
        name="proj_c",
    )(xb, w_c, gq2, gkv2, w_uq, w_ukv, c1, s1)


def _dense_attn_kernel(q_ref, k_ref, v_ref, o_ref, s_ref, smax_ref, m_ref, acc_ref, *, group, tq, tk, n_chunks,
                       unroll):
    q = jnp.concatenate([q_ref[0, g] for g in range(group)], axis=1) if group > 1 else q_ref[0, 0]
    m_ref[...] = jnp.full(m_ref.shape, NEG_INF, F32)
    acc_ref[...] = jnp.zeros(acc_ref.shape, F32)

    def fill(slot, c):
        kc = k_ref[0, 0, pl.ds(pl.multiple_of(c * tk, tk), tk), :]
        s = _dot(kc, q)
        s_ref[slot] = s
        smax_ref[slot] = jnp.max(s, axis=0, keepdims=True)

    def consume(slot, c):
        m_old = m_ref[...]
        m_new = jnp.maximum(m_old, smax_ref[slot])
        p = jnp.exp2(s_ref[slot] - m_new).astype(BF16)
        vc = v_ref[0, 0, c]
        if group > 1:
            pv = jnp.concatenate([_dot(vc, p[:, g * tq:(g + 1) * tq]) for g in range(group)], axis=1)
        else:
            pv = _dot(vc, p)
        acc_ref[...] = acc_ref[...] * jnp.exp2(m_old - m_new) + pv
        m_ref[...] = m_new

    fill(0, 0)

    def group_of_chunks(i, carry):
        c = unroll * i
        for u in range(unroll):
            fill((u + 1) % 2, c + u + 1)
            consume(u % 2, c + u)
        return carry

    lax.fori_loop(0, n_chunks // unroll - 1, group_of_chunks, 0)
    c = n_chunks - unroll
    for u in range(unroll - 1):
        fill((u + 1) % 2, c + u + 1)
        consume(u % 2, c + u)
    consume((unroll - 1) % 2, n_chunks - 1)
    acc = acc_ref[...]
    o = acc[0:HEAD_DIM] / acc[HEAD_DIM:HEAD_DIM + 1]
    for g in range(group):
        o_ref[0, g * HEAD_DIM:(g + 1) * HEAD_DIM, :] = o[:, g * tq:(g + 1) * tq].astype(BF16)


def _dense_attn(qT, k, vT, *, group, tq, tk, unroll, name):
    B, hq, dk, S = qT.shape
    hkv_arrays = k.shape[1]
    n_kv = hq // group
    n_chunks = S // tk
    unroll = min(unroll, n_chunks)
    assert n_chunks % unroll == 0 and unroll >= 2
    kernel = functools.partial(_dense_attn_kernel, group=group, tq=tq, tk=tk, n_chunks=n_chunks, unroll=unroll)
    k_map = (lambda b, h, i: (b, h, 0, 0)) if hkv_arrays > 1 else (lambda b, h, i: (b, 0, 0, 0))
    return pl.pallas_call(
        kernel,
        grid=(B, n_kv, S // tq),
        in_specs=[
            pl.BlockSpec((1, group, dk, tq), lambda b, h, i: (b, h, 0, i)),
            pl.BlockSpec((1, 1, S, dk), k_map),
            pl.BlockSpec((1, 1, n_chunks, V_ROWS, tk), lambda b, h, i: (b, h, 0, 0, 0)),
        ],
        out_specs=pl.BlockSpec((1, group * HEAD_DIM, tq), lambda b, h, i: (b, h, i)),
        out_shape=jax.ShapeDtypeStruct((B, hq * HEAD_DIM, S), BF16),
        scratch_shapes=[pltpu.VMEM((2, tk, group * tq), F32), pltpu.VMEM((2, 1, group * tq), F32),
                        pltpu.VMEM((1, group * tq), F32),
                        pltpu.VMEM((V_ROWS, group * tq), F32)],
        compiler_params=_params(("parallel", "parallel", "parallel")),
        name=name,
    )(qT, k, vT)


def _alibi_window_table():
    n_off = 4 * B_REACH + 1
    off = (jnp.arange(n_off, dtype=jnp.int32) - 2 * B_REACH)[:, None, None] * BQ
    kk = jnp.arange(BQ, dtype=jnp.int32)[None, :, None]
    qq = jnp.arange(BQ, dtype=jnp.int32)[None, None, :]
    dist = jnp.abs(off + kk - qq)
    mult = sum(((dist % dilation == 0) & (dist <= window // 2)).astype(F32)
               for window, dilation in zip(B_WINDOWS, B_DILATIONS))
    slopes = jnp.exp2(-ALIBI_MAX_BIAS * jnp.arange(1, B_HEADS + 1, dtype=F32) / B_HEADS)
    bias = -slopes[:, None, None, None] * dist.astype(F32)[None] + jnp.log(jnp.maximum(mult, 1.0))[None]
    tbl = jnp.where(mult[None] > 0, bias * LOG2E, NEG_INF)
    return tbl.reshape(B_HEADS, n_off * BQ, BQ)


def _window_attn_kernel(q_ref, k_ref, v_ref, t_ref, o_ref, s_ref, smax_ref, *, n_chunks):
    step = pl.program_id(2)

    def window(qq):
        qc = step * B_QCHUNKS + qq
        w0 = jnp.clip(qc - B_REACH, 0, n_chunks - B_WINDOW)
        return w0, pl.multiple_of((w0 - qc + 2 * B_REACH) * BQ, BQ)

    def fill(slot, hd, qq):
        w0, t0 = window(qq)
        kw = k_ref[0, 0, pl.ds(pl.multiple_of(w0 * BQ, BQ), B_WINDOW * BQ), :]
        s = _dot(kw, q_ref[0, hd, :, qq * BQ:(qq + 1) * BQ]) + t_ref[hd, pl.ds(t0, B_WINDOW * BQ), :]
        s_ref[slot] = s
        smax_ref[slot] = jnp.max(s, axis=0, keepdims=True)

    def consume(slot, hd, qq):
        w0, _ = window(qq)
        p = jnp.exp2(s_ref[slot] - smax_ref[slot]).astype(BF16)
        acc = _dot(v_ref[0, hd, w0], p[0:BQ])
        for u in range(1, B_WINDOW):
            acc = acc + _dot(v_ref[0, hd, w0 + u], p[u * BQ:(u + 1) * BQ])
        o_ref[0, hd * HEAD_DIM:(hd + 1) * HEAD_DIM, qq * BQ:(qq + 1) * BQ] = (
            acc[0:HEAD_DIM] / acc[HEAD_DIM:HEAD_DIM + 1]).astype(BF16)

    chains = [(hd, qq) for qq in range(B_QCHUNKS) for hd in range(2)]
    fill(0, *chains[0])
    for n, chain in enumerate(chains):
        if n + 1 < len(chains):
            fill((n + 1) % 2, *chains[n + 1])
        consume(n % 2, *chain)


def _window_attn(qT, k, vT, table):
    B, _, _, S = qT.shape
    n_chunks = S // BQ
    assert n_chunks >= B_WINDOW and n_chunks % B_QCHUNKS == 0
    kernel = functools.partial(_window_attn_kernel, n_chunks=n_chunks)
    return pl.pallas_call(
        kernel,
        grid=(B, B_HEADS // 2, n_chunks // B_QCHUNKS),
        in_specs=[
            pl.BlockSpec((1, 2, 2 * HEAD_DIM, B_QCHUNKS * BQ), lambda b, h, i: (b, h, 0, i)),
            pl.BlockSpec((1, 1, S, 2 * HEAD_DIM), lambda b, h, i: (b, h, 0, 0)),
            pl.BlockSpec((1, 2, n_chunks, V_ROWS, BQ), lambda b, h, i: (b, h, 0, 0, 0)),
            pl.BlockSpec((2, (4 * B_REACH + 1) * BQ, BQ), lambda b, h, i: (h, 0, 0)),
        ],
        out_specs=pl.BlockSpec((1, 2 * HEAD_DIM, B_QCHUNKS * BQ), lambda b, h, i: (b, h, i)),
        out_shape=jax.ShapeDtypeStruct((B, B_W, S), BF16),
        scratch_shapes=[pltpu.VMEM((2, B_WINDOW * BQ, BQ), F32), pltpu.VMEM((2, 1, BQ), F32)],
        compiler_params=_params(("parallel", "parallel", "parallel")),
        name="window_attn",
    )(qT, k, vT, table)


def _out_proj_kernel(a_ref, b_ref, c_ref, w_ref, x_ref, g_ref, beta_ref, of_ref, ob_ref):
    mixT = jnp.concatenate([a_ref[0], b_ref[0], c_ref[0]], axis=0)
    mix = mixT.astype(F32).T.astype(BF16)
    y = _dot(mix, w_ref[...])
    x1 = _layer_norm(ALPHA * x_ref[...] + y, g_ref[...], beta_ref[...])
    of_ref[...] = x1
    ob_ref[...] = x1.astype(BF16)


def _out_proj(aT, bT, cT, w_o, x, g, beta, B, S):
    nt = S // TM
    n = B * S
    row = pl.BlockSpec((TM, D_MODEL), lambda b, i: (b * nt + i, 0))
    vec = pl.BlockSpec((1, D_MODEL), lambda b, i: (0, 0))
    featT = lambda a: pl.BlockSpec((1, a.shape[1], TM), lambda b, i: (b, 0, i))
    return pl.pallas_call(
        _out_proj_kernel,
        grid=(B, nt),
        in_specs=[featT(aT), featT(bT), featT(cT), pl.BlockSpec(w_o.shape, lambda b, i: (0, 0)), row, vec, vec],
        out_specs=[row, row],
        out_shape=[jax.ShapeDtypeStruct((n, D_MODEL), F32), jax.ShapeDtypeStruct((n, D_MODEL), BF16)],
        compiler_params=_params(("parallel", "parallel")),
        name="out_proj",
    )(aT, bT, cT, w_o, x, g.reshape(1, -1), beta.reshape(1, -1))


HALO = 16


def _ffn_up_kernel(xb_ref, prev_ref, next_ref, wg_ref, wv_ref, wc_ref, bc_ref, h_ref, *, n_tiles):
    i = pl.program_id(1)
    xb = xb_ref[...]
    wg = wg_ref[...]
    gate = _dot(xb, wg)
    val = _dot(xb, wv_ref[...])
    g_prev = _dot(prev_ref[...], wg)[HALO - 1:HALO]
    g_next = _dot(next_ref[...], wg)[0:1]
    g_prev = jnp.where(i > 0, g_prev, 0.0)
    g_next = jnp.where(i < n_tiles - 1, g_next, 0.0)
    row = lax.broadcasted_iota(jnp.int32, gate.shape, 0)
    up = jnp.where(row == 0, g_prev, pltpu.roll(gate, 1, axis=0))
    down = jnp.where(row == TM_UP - 1, g_next, pltpu.roll(gate, TM_UP - 1, axis=0))
    wc = wc_ref[...]
    conv = up * wc[0:1] + gate * wc[1:2] + down * wc[2:3] + bc_ref[...]
    act = 0.5 * conv * (1.0 + lax.erf(conv * (2.0 ** -0.5)))
    h_ref[...] = (act * val).astype(BF16)


def _ffn_down_kernel(h_ref, wd_ref, xf_ref, g_ref, beta_ref, of_ref, ob_ref):
    y = _dot(h_ref[...], wd_ref[...])
    x2 = _layer_norm(ALPHA * xf_ref[...] + y, g_ref[...], beta_ref[...])
    of_ref[...] = x2
    ob_ref[...] = x2.astype(BF16)


def _ffn(xb, xf, w_gate, w_val, w_conv, b_conv, w_down, g, beta, B, S):
    n = B * S
    nt = S // TM_UP
    hb = TM_UP // HALO
    last_halo = n // HALO - 1
    hid = pl.pallas_call(
        functools.partial(_ffn_up_kernel, n_tiles=nt),
        grid=(B, nt, D_FF // TF),
        in_specs=[
            pl.BlockSpec((TM_UP, D_MODEL), lambda b, i, j: (b * nt + i, 0)),
            pl.BlockSpec((HALO, D_MODEL), lambda b, i, j: (jnp.maximum((b * nt + i) * hb - 1, 0), 0)),
            pl.BlockSpec((HALO, D_MODEL), lambda b, i, j: (jnp.minimum((b * nt + i + 1) * hb, last_halo), 0)),
            pl.BlockSpec((D_MODEL, TF), lambda b, i, j: (0, j)),
            pl.BlockSpec((D_MODEL, TF), lambda b, i, j: (0, j)),
            pl.BlockSpec((CONV_W, TF), lambda b, i, j: (0, j)),
            pl.BlockSpec((1, TF), lambda b, i, j: (0, j)),
        ],
        out_specs=pl.BlockSpec((TM_UP, TF), lambda b, i, j: (b * nt + i, j)),
        out_shape=jax.ShapeDtypeStruct((n, D_FF), BF16),
        compiler_params=_params(("parallel", "parallel", "parallel")),
        name="ffn_up",
    )(xb, xb, xb, w_gate, w_val, w_conv, b_conv.reshape(1, -1))
    row = pl.BlockSpec((TM_DOWN, D_MODEL), lambda i: (i, 0))
    vec = pl.BlockSpec((1, D_MODEL), lambda i: (0, 0))
    return pl.pallas_call(
        _ffn_down_kernel,
        grid=(n // TM_DOWN,),
        in_specs=[
            pl.BlockSpec((TM_DOWN, D_FF), lambda i: (i, 0)),
            pl.BlockSpec((D_FF, D_MODEL), lambda i: (0, 0), pipeline_mode=pl.Buffered(1)),
            row, vec, vec,
        ],
        out_specs=[row, row],
        out_shape=[jax.ShapeDtypeStruct((n, D_MODEL), F32), jax.ShapeDtypeStruct((n, D_MODEL), BF16)],
        compiler_params=_params(("parallel",)),
        name="ffn_down",
    )(hid, w_down, xf, g.reshape(1, -1), beta.reshape(1, -1))


def _rope_tables(pos, dim):
    inv_freq = jnp.power(ROPE_THETA, -jnp.arange(0, dim, 2, dtype=F32) / dim)
    ang = pos.astype(F32)[:, None] * inv_freq[None, :]
    return jnp.cos(ang).T, jnp.sin(ang).T


def _position_tables(S):
    rows = S // GRID_W
    row = jnp.repeat(jnp.arange(rows, dtype=jnp.int32), GRID_W)
    col = jnp.tile(jnp.arange(GRID_W, dtype=jnp.int32), rows)
    t = jnp.arange(S, dtype=jnp.int32)
    cr, sr = _rope_tables(row, HEAD_DIM // 2)
    cc, sc = _rope_tables(col, HEAD_DIM // 2)
    c1, s1 = _rope_tables(t, C_ROPE)
    return (cr, sr, cc, sc), (c1, s1)


def _prepare_weights(w_in, w_uq, w_ukv, w_o, w_up, w_down):
    o = np.cumsum([0, A_W, A_KV_W, A_KV_W, B_W, B_W, B_W, C_Q_LORA, C_KV_LORA, C_ROPE])
    layers = []
    uq_cols = np.concatenate(
        [np.concatenate([np.arange(h * 96, h * 96 + C_NOPE) for h in range(C_HEADS)]),
         np.concatenate([np.arange(h * 96 + C_NOPE, (h + 1) * 96) for h in range(C_HEADS)])])
    ukv_cols = np.concatenate(
        [np.concatenate([np.arange(h * 128, h * 128 + C_NOPE) for h in range(C_HEADS)]),
         np.concatenate([np.arange(h * 128 + C_NOPE, (h + 1) * 128) for h in range(C_HEADS)])])
    for l in range(DEPTH):
        wi = w_in[l].astype(BF16)
        w_c = jnp.pad(wi[:, o[6]:o[9]], ((0, 0), (0, 128 - C_ROPE)))
        layers.append(dict(
            w_a=wi[:, o[0]:o[3]], w_b=wi[:, o[3]:o[6]], w_c=w_c,
            w_uq=w_uq[l].astype(BF16)[:, uq_cols], w_ukv=w_ukv[l].astype(BF16)[:, ukv_cols],
            w_o=w_o[l].astype(BF16),
            w_gate=w_up[l, :, :D_FF].astype(BF16), w_val=w_up[l, :, D_FF:].astype(BF16),
            w_down=w_down[l].astype(BF16)))
    return layers


def _trunk(x, weights, table, ln_emb_g, ln_emb_b, a_q_norm, a_k_norm, c_q_norm, c_kv_norm,
           ln1_g, ln1_b, w_conv, b_conv, ln2_g, ln2_b):
    B, S, _ = x.shape
    tabs_a, (c1, s1) = _position_tables(S)
    xf, xb = _embed_ln(x.reshape(B * S, D_MODEL), ln_emb_g, ln_emb_b)
    for l in range(DEPTH):
        w = weights[l]
        aq, ak, av = _proj_a(xb, w["w_a"], a_q_norm[l], a_k_norm[l], tabs_a, B, S)
        bq, bk, bv = _proj_b(xb, w["w_b"], B, S)
        cq, ck, cv = _proj_c(xb, w["w_c"], c_q_norm[l], c_kv_norm[l], w["w_uq"], w["w_ukv"], c1, s1, B, S)
        a_out = _dense_attn(aq, ak.reshape(B, 1, S, A_KV_W), av, group=A_GROUP, tq=TQ_A, tk=TK_A,
                            unroll=CHUNKS_PER_TRIP_A, name="attn_a")
        b_out = _window_attn(bq, bk, bv, table)
        c_out = _dense_attn(cq, ck, cv, group=1, tq=TQ_C, tk=TK_C, unroll=CHUNKS_PER_TRIP_C, name="attn_c")
        xf, xb = _out_proj(a_out, b_out, c_out, w["w_o"], xf, ln1_g[l], ln1_b[l], B, S)
        xf, xb = _ffn(xb, xf, w["w_gate"], w["w_val"], w_conv[l], b_conv[l], w["w_down"], ln2_g[l], ln2_b[l], B, S)
    return xf.reshape(B, S, D_MODEL)


def kernel(x_prompt, x_sample, ln_emb_g, ln_emb_b, w_in, a_q_norm, a_k_norm, c_q_norm, c_kv_norm, w_uq, w_ukv,
           w_o, ln1_g, ln1_b, w_up, w_conv, b_conv, w_down, ln2_g, ln2_b):
    weights = _prepare_weights(w_in, w_uq, w_ukv, w_o, w_up, w_down)
    table = _alibi_window_table()
    rest = (ln_emb_g, ln_emb_b, a_q_norm, a_k_norm, c_q_norm, c_kv_norm, ln1_g, ln1_b, w_conv, b_conv, ln2_g, ln2_b)
    return (_trunk(x_prompt, weights, table, *rest), _trunk(x_sample, weights, table, *rest))
```

```python
import functools
import math

import numpy as np
import jax
import jax.numpy as jnp
from jax import lax
from jax.experimental import pallas as pl
from jax.experimental.pallas import tpu as pltpu

F32 = jnp.float32
BF16 = jnp.bfloat16

D_MODEL = 2048
DEPTH = 2
HEAD_DIM = 64
GRID_W = 64
ROPE_THETA = 10000.0
NORM_EPS = 1e-6
NEG_INF = -1e30
A_Q_HEADS = 12
A_KV_HEADS = 4
A_GROUP = A_Q_HEADS // A_KV_HEADS
B_HEADS = 12
B_WINDOWS = (128, 512, 2048)
B_DILATIONS = (1, 4, 16)
ALIBI_MAX_BIAS = 8.0
C_HEADS = 8
C_NOPE = 64
C_ROPE = 32
C_V = 64
C_Q_LORA = 512
C_KV_LORA = 256
A_W = A_Q_HEADS * HEAD_DIM
A_KV_W = A_KV_HEADS * HEAD_DIM
B_W = B_HEADS * HEAD_DIM
C_W = C_HEADS * C_V
D_FF = 5632
CONV_W = 3
ALPHA = (2.0 * DEPTH) ** 0.25

ROPE_FREQS = 16
V_ROWS = HEAD_DIM + 16
C_QK = 128

VMEM_LIMIT_BYTES = 56 * 1024 * 1024

TM = 512
TK_A = 512
TK_C = 512
CHUNKS_PER_TRIP_C = 8
TQ_C = 1024
BQ = 256
B_REACH = 4
B_WINDOW = 2 * B_REACH + 1
B_QCHUNKS = 8
LOG2E = math.log2(math.e)
TF = 512
TM_UP = 1024
TM_DOWN = 256


def _params(sem):
    return pltpu.CompilerParams(dimension_semantics=sem, vmem_limit_bytes=VMEM_LIMIT_BYTES)


def _dot(a, b):
    return jnp.dot(a, b, preferred_element_type=F32)


def _layer_norm(y, g, b):
    mu = jnp.mean(y, axis=-1, keepdims=True)
    yc = y - mu
    var = jnp.mean(yc * yc, axis=-1, keepdims=True)
    return yc * lax.rsqrt(var + NORM_EPS) * g + b


def _value_rows(vT):
    n = vT.shape[1]
    row = lax.broadcasted_iota(jnp.int32, (V_ROWS - HEAD_DIM, n), 0)
    ones = jnp.where(row == 0, 1.0, 0.0).astype(F32)
    return jnp.concatenate([vT, ones], axis=0).astype(BF16)


def _rope_rows(x1, x2, c, s):
    return x1 * c - x2 * s, x2 * c + x1 * s


def _ln_kernel(x_ref, g_ref, b_ref, of_ref, ob_ref):
    y = _layer_norm(x_ref[...], g_ref[...], b_ref[...])
    of_ref[...] = y
    ob_ref[...] = y.astype(BF16)


def _embed_ln(x, g, b):
    n = x.shape[0]
    row = pl.BlockSpec((TM, D_MODEL), lambda i: (i, 0))
    vec = pl.BlockSpec((1, D_MODEL), lambda i: (0, 0))
    return pl.pallas_call(
        _ln_kernel,
        grid=(n // TM,),
        in_specs=[row, vec, vec],
        out_specs=[row, row],
        out_shape=[jax.ShapeDtypeStruct((n, D_MODEL), F32), jax.ShapeDtypeStruct((n, D_MODEL), BF16)],
        compiler_params=_params(("parallel",)),
        name="embed_ln",
    )(x, g.reshape(1, -1), b.reshape(1, -1))


def _proj_a_kernel(x_ref, w_ref, gq_ref, gk_ref, cr_ref, sr_ref, cc_ref, sc_ref, q_ref, k_ref, v_ref):
    h = _dot(x_ref[...], w_ref[...])
    hT = h.T
    cr, sr, cc, sc = cr_ref[...], sr_ref[...], cc_ref[...], sc_ref[...]

    def norm_rope(blk, g):
        ms = jnp.mean(blk * blk, axis=0, keepdims=True)
        y = blk * lax.rsqrt(ms + NORM_EPS) * g
        r1, r2 = _rope_rows(y[0:16], y[16:32], cr, sr)
        r3, r4 = _rope_rows(y[32:48], y[48:64], cc, sc)
        return jnp.concatenate([r1, r2, r3, r4], axis=0)

    gq, gk = gq_ref[...], gk_ref[...]
    zeros = jnp.zeros((A_KV_W, TM), BF16)
    for hq in range(A_Q_HEADS):
        kvh = hq // A_GROUP
        qh = (norm_rope(hT[hq * HEAD_DIM:(hq + 1) * HEAD_DIM], gq) * (HEAD_DIM ** -0.5 * LOG2E)).astype(BF16)
        q_ref[0, hq] = zeros
        q_ref[0, hq, kvh * HEAD_DIM:(kvh + 1) * HEAD_DIM, :] = qh
    kT = jnp.concatenate(
        [norm_rope(hT[A_W + j * HEAD_DIM:A_W + (j + 1) * HEAD_DIM], gk) for j in range(A_KV_HEADS)], axis=0)
    k_ref[0] = kT.T.astype(BF16)
    v0 = A_W + A_KV_W
    for j in range(A_KV_HEADS):
        rows = _value_rows(hT[v0 + j * HEAD_DIM:v0 + (j + 1) * HEAD_DIM])
        for c in range(TM // TK_A):
            v_ref[0, j, c] = rows[:, c * TK_A:(c + 1) * TK_A]


def _proj_a(xb, w_a, gq, gk, tabs, B, S):
    cr, sr, cc, sc = tabs
    nt = S // TM
    tab = pl.BlockSpec((ROPE_FREQS, TM), lambda b, i: (0, i))
    gain = pl.BlockSpec((HEAD_DIM, 1), lambda b, i: (0, 0))
    return pl.pallas_call(
        _proj_a_kernel,
        grid=(B, nt),
        in_specs=[
            pl.BlockSpec((TM, D_MODEL), lambda b, i: (b * nt + i, 0)),
            pl.BlockSpec(w_a.shape, lambda b, i: (0, 0)),
            gain, gain, tab, tab, tab, tab,
        ],
        out_specs=[
            pl.BlockSpec((1, A_Q_HEADS, A_KV_W, TM), lambda b, i: (b, 0, 0, i)),
            pl.BlockSpec((1, TM, A_KV_W), lambda b, i: (b, i, 0)),
            pl.BlockSpec((1, A_KV_HEADS, TM // TK_A, V_ROWS, TK_A), lambda b, i: (b, 0, i, 0, 0)),
        ],
        out_shape=[
            jax.ShapeDtypeStruct((B, A_Q_HEADS, A_KV_W, S), BF16),
            jax.ShapeDtypeStruct((B, S, A_KV_W), BF16),
            jax.ShapeDtypeStruct((B, A_KV_HEADS, S // TK_A, V_ROWS, TK_A), BF16),
        ],
        compiler_params=_params(("parallel", "parallel")),
        name="proj_a",
    )(xb, w_a, gq.reshape(-1, 1), gk.reshape(-1, 1), cr, sr, cc, sc)


def _proj_b_kernel(x_ref, w_ref, q_ref, k_ref, v_ref):
    h = _dot(x_ref[...], w_ref[...])
    hq = h[:, 0:B_W]
    qT = (hq * (HEAD_DIM ** -0.5 * LOG2E)).T.astype(BF16)
    zeros = jnp.zeros((HEAD_DIM, TM), BF16)
    for hd in range(B_HEADS):
        lo = (hd % 2) * HEAD_DIM
        q_ref[0, hd, lo:lo + HEAD_DIM, :] = qT[hd * HEAD_DIM:(hd + 1) * HEAD_DIM]
        q_ref[0, hd, HEAD_DIM - lo:2 * HEAD_DIM - lo, :] = zeros
    for p in range(B_HEADS // 2):
        k_ref[0, p] = h[:, B_W + p * 128:B_W + (p + 1) * 128].astype(BF16)
    vT = h[:, 2 * B_W:3 * B_W].T
    for hd in range(B_HEADS):
        rows = _value_rows(vT[hd * HEAD_DIM:(hd + 1) * HEAD_DIM])
        for c in range(TM // BQ):
            v_ref[0, hd, c] = rows[:, c * BQ:(c + 1) * BQ]


def _proj_b(xb, w_b, B, S):
    nt = S // TM
    return pl.pallas_call(
        _proj_b_kernel,
        grid=(B, nt),
        in_specs=[
            pl.BlockSpec((TM, D_MODEL), lambda b, i: (b * nt + i, 0)),
            pl.BlockSpec(w_b.shape, lambda b, i: (0, 0)),
        ],
        out_specs=[
            pl.BlockSpec((1, B_HEADS, 2 * HEAD_DIM, TM), lambda b, i: (b, 0, 0, i)),
            pl.BlockSpec((1, B_HEADS // 2, TM, 2 * HEAD_DIM), lambda b, i: (b, 0, i, 0)),
            pl.BlockSpec((1, B_HEADS, TM // BQ, V_ROWS, BQ), lambda b, i: (b, 0, i, 0, 0)),
        ],
        out_shape=[
            jax.ShapeDtypeStruct((B, B_HEADS, 2 * HEAD_DIM, S), BF16),
            jax.ShapeDtypeStruct((B, B_HEADS // 2, S, 2 * HEAD_DIM), BF16),
            jax.ShapeDtypeStruct((B, B_HEADS, S // BQ, V_ROWS, BQ), BF16),
        ],
        compiler_params=_params(("parallel", "parallel")),
        name="proj_b",
    )(xb, w_b)


def _proj_c_kernel(x_ref, w_ref, gq_ref, gkv_ref, wuq_ref, wukv_ref, c1_ref, s1_ref, q_ref, k_ref, v_ref):
    h = _dot(x_ref[...], w_ref[...])
    c1, s1 = c1_ref[...], s1_ref[...]

    def rms(x, g):
        ms = jnp.mean(x * x, axis=-1, keepdims=True)
        return (x * lax.rsqrt(ms + NORM_EPS) * g).astype(BF16)

    cq = rms(h[:, 0:C_Q_LORA], gq_ref[...])
    ckv = rms(h[:, C_Q_LORA:C_Q_LORA + C_KV_LORA], gkv_ref[...])
    qcT = _dot(cq, wuq_ref[...]).T * ((C_NOPE + C_ROPE) ** -0.5 * LOG2E)
    kvT = _dot(ckv, wukv_ref[...]).T
    krT = h[:, C_Q_LORA + C_KV_LORA:].T
    kr1, kr2 = _rope_rows(krT[0:16], krT[16:32], c1, s1)
    pad = jnp.zeros((C_QK - C_NOPE - C_ROPE, TM), F32)
    nope_w = C_HEADS * C_NOPE
    for hd in range(C_HEADS):
        r0 = nope_w + hd * C_ROPE
        qr1, qr2 = _rope_rows(qcT[r0:r0 + 16], qcT[r0 + 16:r0 + 32], c1, s1)
        q_ref[0, hd] = jnp.concatenate(
            [qcT[hd * C_NOPE:(hd + 1) * C_NOPE], qr1, qr2, pad], axis=0).astype(BF16)
        kT = jnp.concatenate([kvT[hd * C_NOPE:(hd + 1) * C_NOPE], kr1, kr2, pad], axis=0)
        k_ref[0, hd] = kT.T.astype(BF16)
        rows = _value_rows(kvT[nope_w + hd * C_V:nope_w + (hd + 1) * C_V])
        for c in range(TM // TK_C):
            v_ref[0, hd, c] = rows[:, c * TK_C:(c + 1) * TK_C]


def _proj_c(xb, w_c, gq, gkv, w_uq, w_ukv, c1, s1, B, S):
    nt = S // TM
    tab = pl.BlockSpec((ROPE_FREQS, TM), lambda b, i: (0, i))
    full = lambda a: pl.BlockSpec(a.shape, lambda b, i: (0,) * a.ndim)
    gq2, gkv2 = gq.reshape(1, -1), gkv.reshape(1, -1)
    return pl.pallas_call(
        _proj_c_kernel,
        grid=(B, nt),
        in_specs=[
            pl.BlockSpec((TM, D_MODEL), lambda b, i: (b * nt + i, 0)),
            full(w_c), full(gq2), full(gkv2), full(w_uq), full(w_ukv), tab, tab,
        ],
        out_specs=[
            pl.BlockSpec((1, C_HEADS, C_QK, TM), lambda b, i: (b, 0, 0, i)),
            pl.BlockSpec((1, C_HEADS, TM, C_QK), lambda b, i: (b, 0, i, 0)),
            pl.BlockSpec((1, C_HEADS, TM // TK_C, V_ROWS, TK_C), lambda b, i: (b, 0, i, 0, 0)),
        ],
        out_shape=[
            jax.ShapeDtypeStruct((B, C_HEADS, C_QK, S), BF16),
            jax.ShapeDtypeStruct((B, C_HEADS, S, C_QK), BF16),
            jax.ShapeDtypeStruct((B, C_HEADS, S // TK_C, V_ROWS, TK_C), BF16),
        ],
        compiler_params=_params(("parallel", "parallel")),
        name="proj_c",
    )(xb, w_c, gq2, gkv2, w_uq, w_ukv, c1, s1)


def _dense_attn_kernel(q_ref, k_ref, v_ref, o_ref, s_ref, smax_ref, m_ref, acc_ref, *, group, tq, tk, n_chunks,
                       unroll):
    q = jnp.concatenate([q_ref[0, g] for g in range(group)], axis=1) if group > 1 else q_ref[0, 0]
    m_ref[...] = jnp.full(m_ref.shape, NEG_INF, F32)
    acc_ref[...] = jnp.zeros(acc_ref.shape, F32)

    def fill(slot, c):
        kc = k_ref[0, 0, pl.ds(pl.multiple_of(c * tk, tk), tk), :]
        s = _dot(kc, q)
        s_ref[slot] = s
        smax_ref[slot] = jnp.max(s, axis=0, keepdims=True)

    def consume(slot, c):
        m_old = m_ref[...]
        m_new = jnp.maximum(m_old, smax_ref[slot])
        p = jnp.exp2(s_ref[slot] - m_new).astype(BF16)
        vc = v_ref[0, 0, c]
        if group > 1:
            pv = jnp.concatenate([_dot(vc, p[:, g * tq:(g + 1) * tq]) for g in range(group)], axis=1)
        else:
            pv = _dot(vc, p)
        acc_ref[...] = acc_ref[...] * jnp.exp2(m_old - m_new) + pv
        m_ref[...] = m_new

    fill(0, 0)

    def group_of_chunks(i, carry):
        c = unroll * i
        for u in range(unroll):
            fill((u + 1) % 2, c + u + 1)
            consume(u % 2, c + u)
        return carry

    lax.fori_loop(0, n_chunks // unroll - 1, group_of_chunks, 0)
    c = n_chunks - unroll
    for u in range(unroll - 1):
        fill((u + 1) % 2, c + u + 1)
        consume(u % 2, c + u)
    consume((unroll - 1) % 2, n_chunks - 1)
    acc = acc_ref[...]
    o = acc[0:HEAD_DIM] / acc[HEAD_DIM:HEAD_DIM + 1]
    for g in range(group):
        o_ref[0, g * HEAD_DIM:(g + 1) * HEAD_DIM, :] = o[:, g * tq:(g + 1) * tq].astype(BF16)


def _mixer_a_tiling(n_chunks):
    return (1, 1024, 8) if n_chunks > 8 else (A_GROUP, 512, 4)


def _dense_attn(qT, k, vT, *, group, tq, tk, unroll, name):
    B, hq, dk, S = qT.shape
    hkv_arrays = k.shape[1]
    n_kv = hq // group
    v_share = n_kv // vT.shape[1]
    n_chunks = S // tk
    if n_chunks <= unroll:
        unroll = 2
    assert n_chunks % unroll == 0 and unroll >= 2
    kernel = functools.partial(_dense_attn_kernel, group=group, tq=tq, tk=tk, n_chunks=n_chunks, unroll=unroll)
    k_map = (lambda b, h, i: (b, h, 0, 0)) if hkv_arrays > 1 else (lambda b, h, i: (b, 0, 0, 0))
    return pl.pallas_call(
        kernel,
        grid=(B, n_kv, S // tq),
        in_specs=[
            pl.BlockSpec((1, group, dk, tq), lambda b, h, i: (b, h, 0, i)),
            pl.BlockSpec((1, 1, S, dk), k_map),
            pl.BlockSpec((1, 1, n_chunks, V_ROWS, tk), lambda b, h, i: (b, h // v_share, 0, 0, 0)),
        ],
        out_specs=pl.BlockSpec((1, group * HEAD_DIM, tq), lambda b, h, i: (b, h, i)),
        out_shape=jax.ShapeDtypeStruct((B, hq * HEAD_DIM, S), BF16),
        scratch_shapes=[pltpu.VMEM((2, tk, group * tq), F32), pltpu.VMEM((2, 1, group * tq), F32),
                        pltpu.VMEM((1, group * tq), F32),
                        pltpu.VMEM((V_ROWS, group * tq), F32)],
        compiler_params=_params(("parallel", "parallel", "parallel")),
        name=name,
    )(qT, k, vT)


def _alibi_window_table():
    n_off = 4 * B_REACH + 1
    off = (jnp.arange(n_off, dtype=jnp.int32) - 2 * B_REACH)[:, None, None] * BQ
    kk = jnp.arange(BQ, dtype=jnp.int32)[None, :, None]
    qq = jnp.arange(BQ, dtype=jnp.int32)[None, None, :]
    dist = jnp.abs(off + kk - qq)
    mult = sum(((dist % dilation == 0) & (dist <= window // 2)).astype(F32)
               for window, dilation in zip(B_WINDOWS, B_DILATIONS))
    slopes = jnp.exp2(-ALIBI_MAX_BIAS * jnp.arange(1, B_HEADS + 1, dtype=F32) / B_HEADS)
    bias = -slopes[:, None, None, None] * dist.astype(F32)[None] + jnp.log(jnp.maximum(mult, 1.0))[None]
    tbl = jnp.where(mult[None] > 0, bias * LOG2E, NEG_INF)
    return tbl.reshape(B_HEADS, n_off * BQ, BQ)


def _window_attn_kernel(q_ref, k_ref, v_ref, t_ref, o_ref, s_ref, smax_ref, *, n_chunks):
    step = pl.program_id(2)

    def window(qq):
        qc = step * B_QCHUNKS + qq
        w0 = jnp.clip(qc - B_REACH, 0, n_chunks - B_WINDOW)
        return w0, pl.multiple_of((w0 - qc + 2 * B_REACH) * BQ, BQ)

    def fill(slot, hd, qq):
        w0, t0 = window(qq)
        kw = k_ref[0, 0, pl.ds(pl.multiple_of(w0 * BQ, BQ), B_WINDOW * BQ), :]
        s = _dot(kw, q_ref[0, hd, :, qq * BQ:(qq + 1) * BQ]) + t_ref[hd, pl.ds(t0, B_WINDOW * BQ), :]
        s_ref[slot] = s
        smax_ref[slot] = jnp.max(s, axis=0, keepdims=True)

    def consume(slot, hd, qq):
        w0, _ = window(qq)
        p = jnp.exp2(s_ref[slot] - smax_ref[slot]).astype(BF16)
        acc = _dot(v_ref[0, hd, w0], p[0:BQ])
        for u in range(1, B_WINDOW):
            acc = acc + _dot(v_ref[0, hd, w0 + u], p[u * BQ:(u + 1) * BQ])
        o_ref[0, hd * HEAD_DIM:(hd + 1) * HEAD_DIM, qq * BQ:(qq + 1) * BQ] = (
            acc[0:HEAD_DIM] / acc[HEAD_DIM:HEAD_DIM + 1]).astype(BF16)

    chains = [(hd, qq) for qq in range(B_QCHUNKS) for hd in range(2)]
    fill(0, *chains[0])
    for n, chain in enumerate(chains):
        if n + 1 < len(chains):
            fill((n + 1) % 2, *chains[n + 1])
        consume(n % 2, *chain)


def _window_attn(qT, k, vT, table):
    B, _, _, S = qT.shape
    n_chunks = S // BQ
    assert n_chunks >= B_WINDOW and n_chunks % B_QCHUNKS == 0
    kernel = functools.partial(_window_attn_kernel, n_chunks=n_chunks)
    return pl.pallas_call(
        kernel,
        grid=(B, B_HEADS // 2, n_chunks // B_QCHUNKS),
        in_specs=[
            pl.BlockSpec((1, 2, 2 * HEAD_DIM, B_QCHUNKS * BQ), lambda b, h, i: (b, h, 0, i)),
            pl.BlockSpec((1, 1, S, 2 * HEAD_DIM), lambda b, h, i: (b, h, 0, 0)),
            pl.BlockSpec((1, 2, n_chunks, V_ROWS, BQ), lambda b, h, i: (b, h, 0, 0, 0)),
            pl.BlockSpec((2, (4 * B_REACH + 1) * BQ, BQ), lambda b, h, i: (h, 0, 0)),
        ],
        out_specs=pl.BlockSpec((1, 2 * HEAD_DIM, B_QCHUNKS * BQ), lambda b, h, i: (b, h, i)),
        out_shape=jax.ShapeDtypeStruct((B, B_W, S), BF16),
        scratch_shapes=[pltpu.VMEM((2, B_WINDOW * BQ, BQ), F32), pltpu.VMEM((2, 1, BQ), F32)],
        compiler_params=_params(("parallel", "parallel", "parallel")),
        name="window_attn",
    )(qT, k, vT, table)


def _out_proj_kernel(a_ref, b_ref, c_ref, w_ref, x_ref, g_ref, beta_ref, of_ref, ob_ref):
    mixT = jnp.concatenate([a_ref[0], b_ref[0], c_ref[0]], axis=0)
    mix = mixT.astype(F32).T.astype(BF16)
    y = _dot(mix, w_ref[...])
    x1 = _layer_norm(ALPHA * x_ref[...] + y, g_ref[...], beta_ref[...])
    of_ref[...] = x1
    ob_ref[...] = x1.astype(BF16)


def _out_proj(aT, bT, cT, w_o, x, g, beta, B, S):
    nt = S // TM
    n = B * S
    row = pl.BlockSpec((TM, D_MODEL), lambda b, i: (b * nt + i, 0))
    vec = pl.BlockSpec((1, D_MODEL), lambda b, i: (0, 0))
    featT = lambda a: pl.BlockSpec((1, a.shape[1], TM), lambda b, i: (b, 0, i))
    return pl.pallas_call(
        _out_proj_kernel,
        grid=(B, nt),
        in_specs=[featT(aT), featT(bT), featT(cT), pl.BlockSpec(w_o.shape, lambda b, i: (0, 0)), row, vec, vec],
        out_specs=[row, row],
        out_shape=[jax.ShapeDtypeStruct((n, D_MODEL), F32), jax.ShapeDtypeStruct((n, D_MODEL), BF16)],
        compiler_params=_params(("parallel", "parallel")),
        name="out_proj",
    )(aT, bT, cT, w_o, x, g.reshape(1, -1), beta.reshape(1, -1))


HALO = 16


def _ffn_up_kernel(xb_ref, prev_ref, next_ref, wg_ref, wv_ref, wc_ref, bc_ref, h_ref, *, n_tiles):
    i = pl.program_id(1)
    xb = xb_ref[...]
    wg = wg_ref[...]
    gate = _dot(xb, wg)
    val = _dot(xb, wv_ref[...])
    g_prev = _dot(prev_ref[...], wg)[HALO - 1:HALO]
    g_next = _dot(next_ref[...], wg)[0:1]
    g_prev = jnp.where(i > 0, g_prev, 0.0)
    g_next = jnp.where(i < n_tiles - 1, g_next, 0.0)
    row = lax.broadcasted_iota(jnp.int32, gate.shape, 0)
    up = jnp.where(row == 0, g_prev, pltpu.roll(gate, 1, axis=0))
    down = jnp.where(row == TM_UP - 1, g_next, pltpu.roll(gate, TM_UP - 1, axis=0))
    wc = wc_ref[...]
    conv = up * wc[0:1] + gate * wc[1:2] + down * wc[2:3] + bc_ref[...]
    act = 0.5 * conv * (1.0 + lax.erf(conv * (2.0 ** -0.5)))
    h_ref[...] = (act * val).astype(BF16)


def _ffn_down_kernel(h_ref, wd_ref, xf_ref, g_ref, beta_ref, of_ref, ob_ref):
    y = _dot(h_ref[...], wd_ref[...])
    x2 = _layer_norm(ALPHA * xf_ref[...] + y, g_ref[...], beta_ref[...])
    of_ref[...] = x2
    ob_ref[...] = x2.astype(BF16)


def _ffn(xb, xf, w_gate, w_val, w_conv, b_conv, w_down, g, beta, B, S):
    n = B * S
    nt = S // TM_UP
    hb = TM_UP // HALO
    last_halo = n // HALO - 1
    hid = pl.pallas_call(
        functools.partial(_ffn_up_kernel, n_tiles=nt),
        grid=(B, nt, D_FF // TF),
        in_specs=[
            pl.BlockSpec((TM_UP, D_MODEL), lambda b, i, j: (b * nt + i, 0)),
            pl.BlockSpec((HALO, D_MODEL), lambda b, i, j: (jnp.maximum((b * nt + i) * hb - 1, 0), 0)),
            pl.BlockSpec((HALO, D_MODEL), lambda b, i, j: (jnp.minimum((b * nt + i + 1) * hb, last_halo), 0)),
            pl.BlockSpec((D_MODEL, TF), lambda b, i, j: (0, j)),
            pl.BlockSpec((D_MODEL, TF), lambda b, i, j: (0, j)),
            pl.BlockSpec((CONV_W, TF), lambda b, i, j: (0, j)),
            pl.BlockSpec((1, TF), lambda b, i, j: (0, j)),
        ],
        out_specs=pl.BlockSpec((TM_UP, TF), lambda b, i, j: (b * nt + i, j)),
        out_shape=jax.ShapeDtypeStruct((n, D_FF), BF16),
        compiler_params=_params(("parallel", "parallel", "parallel")),
        name="ffn_up",
    )(xb, xb, xb, w_gate, w_val, w_conv, b_conv.reshape(1, -1))
    row = pl.BlockSpec((TM_DOWN, D_MODEL), lambda i: (i, 0))
    vec = pl.BlockSpec((1, D_MODEL), lambda i: (0, 0))
    return pl.pallas_call(
        _ffn_down_kernel,
        grid=(n // TM_DOWN,),
        in_specs=[
            pl.BlockSpec((TM_DOWN, D_FF), lambda i: (i, 0)),
            pl.BlockSpec((D_FF, D_MODEL), lambda i: (0, 0), pipeline_mode=pl.Buffered(1)),
            row, vec, vec,
        ],
        out_specs=[row, row],
        out_shape=[jax.ShapeDtypeStruct((n, D_MODEL), F32), jax.ShapeDtypeStruct((n, D_MODEL), BF16)],
        compiler_params=_params(("parallel",)),
        name="ffn_down",
    )(hid, w_down, xf, g.reshape(1, -1), beta.reshape(1, -1))


def _rope_tables(pos, dim):
    inv_freq = jnp.power(ROPE_THETA, -jnp.arange(0, dim, 2, dtype=F32) / dim)
    ang = pos.astype(F32)[:, None] * inv_freq[None, :]
    return jnp.cos(ang).T, jnp.sin(ang).T


def _position_tables(S):
    rows = S // GRID_W
    row = jnp.repeat(jnp.arange(rows, dtype=jnp.int32), GRID_W)
    col = jnp.tile(jnp.arange(GRID_W, dtype=jnp.int32), rows)
    t = jnp.arange(S, dtype=jnp.int32)
    cr, sr = _rope_tables(row, HEAD_DIM // 2)
    cc, sc = _rope_tables(col, HEAD_DIM // 2)
    c1, s1 = _rope_tables(t, C_ROPE)
    return (cr, sr, cc, sc), (c1, s1)


def _prepare_weights(w_in, w_uq, w_ukv, w_o, w_up, w_down):
    o = np.cumsum([0, A_W, A_KV_W, A_KV_W, B_W, B_W, B_W, C_Q_LORA, C_KV_LORA, C_ROPE])
    layers = []
    uq_cols = np.concatenate(
        [np.concatenate([np.arange(h * 96, h * 96 + C_NOPE) for h in range(C_HEADS)]),
         np.concatenate([np.arange(h * 96 + C_NOPE, (h + 1) * 96) for h in range(C_HEADS)])])
    ukv_cols = np.concatenate(
        [np.concatenate([np.arange(h * 128, h * 128 + C_NOPE) for h in range(C_HEADS)]),
         np.concatenate([np.arange(h * 128 + C_NOPE, (h + 1) * 128) for h in range(C_HEADS)])])
    for l in range(DEPTH):
        wi = w_in[l].astype(BF16)
        w_c = jnp.pad(wi[:, o[6]:o[9]], ((0, 0), (0, 128 - C_ROPE)))
        layers.append(dict(
            w_a=wi[:, o[0]:o[3]], w_b=wi[:, o[3]:o[6]], w_c=w_c,
            w_uq=w_uq[l].astype(BF16)[:, uq_cols], w_ukv=w_ukv[l].astype(BF16)[:, ukv_cols],
            w_o=w_o[l].astype(BF16),
            w_gate=w_up[l, :, :D_FF].astype(BF16), w_val=w_up[l, :, D_FF:].astype(BF16),
            w_down=w_down[l].astype(BF16)))
    return layers


def _trunk(x, weights, table, ln_emb_g, ln_emb_b, a_q_norm, a_k_norm, c_q_norm, c_kv_norm,
           ln1_g, ln1_b, w_conv, b_conv, ln2_g, ln2_b):
    B, S, _ = x.shape
    tabs_a, (c1, s1) = _position_tables(S)
    xf, xb = _embed_ln(x.reshape(B * S, D_MODEL), ln_emb_g, ln_emb_b)
    for l in range(DEPTH):
        w = weights[l]
        aq, ak, av = _proj_a(xb, w["w_a"], a_q_norm[l], a_k_norm[l], tabs_a, B, S)
        bq, bk, bv = _proj_b(xb, w["w_b"], B, S)
        cq, ck, cv = _proj_c(xb, w["w_c"], c_q_norm[l], c_kv_norm[l], w["w_uq"], w["w_ukv"], c1, s1, B, S)
        a_group, a_tq, a_unroll = _mixer_a_tiling(S // TK_A)
        a_out = _dense_attn(aq, ak.reshape(B, 1, S, A_KV_W), av, group=a_group, tq=a_tq, tk=TK_A,
                            unroll=a_unroll, name="attn_a")
        b_out = _window_attn(bq, bk, bv, table)
        c_out = _dense_attn(cq, ck, cv, group=1, tq=TQ_C, tk=TK_C, unroll=CHUNKS_PER_TRIP_C, name="attn_c")
        xf, xb = _out_proj(a_out, b_out, c_out, w["w_o"], xf, ln1_g[l], ln1_b[l], B, S)
        xf, xb = _ffn(xb, xf, w["w_gate"], w["w_val"], w_conv[l], b_conv[l], w["w_down"], ln2_g[l], ln2_b[l], B, S)
    return xf.reshape(B, S, D_MODEL)


def kernel(x_prompt, x_sample, ln_emb_g, ln_emb_b, w_in, a_q_norm, a_k_norm, c_q_norm, c_kv_norm, w_uq, w_ukv,
           w_o, ln1_g, ln1_b, w_up, w_conv, b_conv, w_down, ln2_g, ln2_b):
    weights = _prepare_weights(w_in, w_uq, w_ukv, w_o, w_up, w_down)
    table = _alibi_window_table()
    rest = (ln_emb_g, ln_emb_b, a_q_norm, a_k_norm, c_q_norm, c_kv_norm, ln1_g, ln1_b, w_conv, b_conv, ln2_g, ln2_b)
    return (_trunk(x_prompt, weights, table, *rest), _trunk(x_sample, weights, table, *rest))
```

```python
import functools
import math

import numpy as np
import jax
import jax.numpy as jnp
from jax import lax
from jax.experimental import pallas as pl
from jax.experimental.pallas import tpu as pltpu

F32 = jnp.float32
BF16 = jnp.bfloat16

D_MODEL = 2048
DEPTH = 2
HEAD_DIM = 64
GRID_W = 64
ROPE_THETA = 10000.0
NORM_EPS = 1e-6
NEG_INF = -1e30
A_Q_HEADS = 12
A_KV_HEADS = 4
A_GROUP = A_Q_HEADS // A_KV_HEADS
B_HEADS = 12
B_WINDOWS = (128, 512, 2048)
B_DILATIONS = (1, 4, 16)
ALIBI_MAX_BIAS = 8.0
C_HEADS = 8
C_NOPE = 64
C_ROPE = 32
C_V = 64
C_Q_LORA = 512
C_KV_LORA = 256
A_W = A_Q_HEADS * HEAD_DIM
A_KV_W = A_KV_HEADS * HEAD_DIM
B_W = B_HEADS * HEAD_DIM
C_W = C_HEADS * C_V
D_FF = 5632
CONV_W = 3
ALPHA = (2.0 * DEPTH) ** 0.25

ROPE_FREQS = 16
V_ROWS = HEAD_DIM + 16
C_QK = 128

VMEM_LIMIT_BYTES = 56 * 1024 * 1024

TM = 512
TK_A = 512
TK_C = 512
TQ_A = 512
CHUNKS_PER_TRIP_C = 8
TQ_C = 1024
BQ = 256
B_REACH = 4
B_WINDOW = 2 * B_REACH + 1
B_QCHUNKS = 8
LOG2E = math.log2(math.e)
TF = 512
TM_UP = 1024
TM_DOWN = 256


def _params(sem):
    return pltpu.CompilerParams(dimension_semantics=sem, vmem_limit_bytes=VMEM_LIMIT_BYTES)


def _dot(a, b):
    return jnp.dot(a, b, preferred_element_type=F32)


def _layer_norm(y, g, b):
    mu = jnp.mean(y, axis=-1, keepdims=True)
    yc = y - mu
    var = jnp.mean(yc * yc, axis=-1, keepdims=True)
    return yc * lax.rsqrt(var + NORM_EPS) * g + b


def _value_rows(vT):
    n = vT.shape[1]
    row = lax.broadcasted_iota(jnp.int32, (V_ROWS - HEAD_DIM, n), 0)
    ones = jnp.where(row == 0, 1.0, 0.0).astype(F32)
    return jnp.concatenate([vT, ones], axis=0).astype(BF16)


def _rope_rows(x1, x2, c, s):
    return x1 * c - x2 * s, x2 * c + x1 * s


def _ln_kernel(x_ref, g_ref, b_ref, of_ref, ob_ref):
    y = _layer_norm(x_ref[...], g_ref[...], b_ref[...])
    of_ref[...] = y
    ob_ref[...] = y.astype(BF16)


def _embed_ln(x, g, b):
    n = x.shape[0]
    row = pl.BlockSpec((TM, D_MODEL), lambda i: (i, 0))
    vec = pl.BlockSpec((1, D_MODEL), lambda i: (0, 0))
    return pl.pallas_call(
        _ln_kernel,
        grid=(n // TM,),
        in_specs=[row, vec, vec],
        out_specs=[row, row],
        out_shape=[jax.ShapeDtypeStruct((n, D_MODEL), F32), jax.ShapeDtypeStruct((n, D_MODEL), BF16)],
        compiler_params=_params(("parallel",)),
        name="embed_ln",
    )(x, g.reshape(1, -1), b.reshape(1, -1))


def _proj_a_kernel(x_ref, w_ref, gq_ref, gk_ref, cr_ref, sr_ref, cc_ref, sc_ref, q_ref, k_ref, v_ref):
    h = _dot(x_ref[...], w_ref[...])
    hT = h.T
    cr, sr, cc, sc = cr_ref[...], sr_ref[...], cc_ref[...], sc_ref[...]

    def norm_rope(blk, g):
        ms = jnp.mean(blk * blk, axis=0, keepdims=True)
        y = blk * lax.rsqrt(ms + NORM_EPS) * g
        r1, r2 = _rope_rows(y[0:16], y[16:32], cr, sr)
        r3, r4 = _rope_rows(y[32:48], y[48:64], cc, sc)
        return jnp.concatenate([r1, r2, r3, r4], axis=0)

    gq, gk = gq_ref[...], gk_ref[...]
    zeros = jnp.zeros((A_KV_W, TM), BF16)
    for hq in range(A_Q_HEADS):
        kvh = hq // A_GROUP
        qh = (norm_rope(hT[hq * HEAD_DIM:(hq + 1) * HEAD_DIM], gq) * (HEAD_DIM ** -0.5 * LOG2E)).astype(BF16)
        q_ref[0, hq] = zeros
        q_ref[0, hq, kvh * HEAD_DIM:(kvh + 1) * HEAD_DIM, :] = qh
    kT = jnp.concatenate(
        [norm_rope(hT[A_W + j * HEAD_DIM:A_W + (j + 1) * HEAD_DIM], gk) for j in range(A_KV_HEADS)], axis=0)
    k_ref[0] = kT.T.astype(BF16)
    v0 = A_W + A_KV_W
    for j in range(A_KV_HEADS):
        rows = _value_rows(hT[v0 + j * HEAD_DIM:v0 + (j + 1) * HEAD_DIM])
        for c in range(TM // TK_A):
            v_ref[0, j, c] = rows[:, c * TK_A:(c + 1) * TK_A]


def _proj_a(xb, w_a, gq, gk, tabs, B, S):
    cr, sr, cc, sc = tabs
    nt = S // TM
    tab = pl.BlockSpec((ROPE_FREQS, TM), lambda b, i: (0, i))
    gain = pl.BlockSpec((HEAD_DIM, 1), lambda b, i: (0, 0))
    return pl.pallas_call(
        _proj_a_kernel,
        grid=(B, nt),
        in_specs=[
            pl.BlockSpec((TM, D_MODEL), lambda b, i: (b * nt + i, 0)),
            pl.BlockSpec(w_a.shape, lambda b, i: (0, 0)),
            gain, gain, tab, tab, tab, tab,
        ],
        out_specs=[
            pl.BlockSpec((1, A_Q_HEADS, A_KV_W, TM), lambda b, i: (b, 0, 0, i)),
            pl.BlockSpec((1, TM, A_KV_W), lambda b, i: (b, i, 0)),
            pl.BlockSpec((1, A_KV_HEADS, TM // TK_A, V_ROWS, TK_A), lambda b, i: (b, 0, i, 0, 0)),
        ],
        out_shape=[
            jax.ShapeDtypeStruct((B, A_Q_HEADS, A_KV_W, S), BF16),
            jax.ShapeDtypeStruct((B, S, A_KV_W), BF16),
            jax.ShapeDtypeStruct((B, A_KV_HEADS, S // TK_A, V_ROWS, TK_A), BF16),
        ],
        compiler_params=_params(("parallel", "parallel")),
        name="proj_a",
    )(xb, w_a, gq.reshape(-1, 1), gk.reshape(-1, 1), cr, sr, cc, sc)


def _proj_b_kernel(x_ref, w_ref, q_ref, k_ref, v_ref):
    h = _dot(x_ref[...], w_ref[...])
    hq = h[:, 0:B_W]
    qT = (hq * (HEAD_DIM ** -0.5 * LOG2E)).T.astype(BF16)
    zeros = jnp.zeros((HEAD_DIM, TM), BF16)
    for hd in range(B_HEADS):
        lo = (hd % 2) * HEAD_DIM
        q_ref[0, hd, lo:lo + HEAD_DIM, :] = qT[hd * HEAD_DIM:(hd + 1) * HEAD_DIM]
        q_ref[0, hd, HEAD_DIM - lo:2 * HEAD_DIM - lo, :] = zeros
    for p in range(B_HEADS // 2):
        k_ref[0, p] = h[:, B_W + p * 128:B_W + (p + 1) * 128].astype(BF16)
    vT = h[:, 2 * B_W:3 * B_W].T
    for hd in range(B_HEADS):
        rows = _value_rows(vT[hd * HEAD_DIM:(hd + 1) * HEAD_DIM])
        for c in range(TM // BQ):
            v_ref[0, hd, c] = rows[:, c * BQ:(c + 1) * BQ]


def _proj_b(xb, w_b, B, S):
    nt = S // TM
    return pl.pallas_call(
        _proj_b_kernel,
        grid=(B, nt),
        in_specs=[
            pl.BlockSpec((TM, D_MODEL), lambda b, i: (b * nt + i, 0)),
            pl.BlockSpec(w_b.shape, lambda b, i: (0, 0)),
        ],
        out_specs=[
            pl.BlockSpec((1, B_HEADS, 2 * HEAD_DIM, TM), lambda b, i: (b, 0, 0, i)),
            pl.BlockSpec((1, B_HEADS // 2, TM, 2 * HEAD_DIM), lambda b, i: (b, 0, i, 0)),
            pl.BlockSpec((1, B_HEADS, TM // BQ, V_ROWS, BQ), lambda b, i: (b, 0, i, 0, 0)),
        ],
        out_shape=[
            jax.ShapeDtypeStruct((B, B_HEADS, 2 * HEAD_DIM, S), BF16),
            jax.ShapeDtypeStruct((B, B_HEADS // 2, S, 2 * HEAD_DIM), BF16),
            jax.ShapeDtypeStruct((B, B_HEADS, S // BQ, V_ROWS, BQ), BF16),
        ],
        compiler_params=_params(("parallel", "parallel")),
        name="proj_b",
    )(xb, w_b)


def _proj_c_kernel(x_ref, w_ref, gq_ref, gkv_ref, wuq_ref, wukv_ref, c1_ref, s1_ref, q_ref, k_ref, v_ref):
    h = _dot(x_ref[...], w_ref[...])
    c1, s1 = c1_ref[...], s1_ref[...]

    def rms(x, g):
        ms = jnp.mean(x * x, axis=-1, keepdims=True)
        return (x * lax.rsqrt(ms + NORM_EPS) * g).astype(BF16)

    cq = rms(h[:, 0:C_Q_LORA], gq_ref[...])
    ckv = rms(h[:, C_Q_LORA:C_Q_LORA + C_KV_LORA], gkv_ref[...])
    qcT = _dot(cq, wuq_ref[...]).T * ((C_NOPE + C_ROPE) ** -0.5 * LOG2E)
    kvT = _dot(ckv, wukv_ref[...]).T
    krT = h[:, C_Q_LORA + C_KV_LORA:].T
    kr1, kr2 = _rope_rows(krT[0:16], krT[16:32], c1, s1)
    pad = jnp.zeros((C_QK - C_NOPE - C_ROPE, TM), F32)
    nope_w = C_HEADS * C_NOPE
    for hd in range(C_HEADS):
        r0 = nope_w + hd * C_ROPE
        qr1, qr2 = _rope_rows(qcT[r0:r0 + 16], qcT[r0 + 16:r0 + 32], c1, s1)
        q_ref[0, hd] = jnp.concatenate(
            [qcT[hd * C_NOPE:(hd + 1) * C_NOPE], qr1, qr2, pad], axis=0).astype(BF16)
        kT = jnp.concatenate([kvT[hd * C_NOPE:(hd + 1) * C_NOPE], kr1, kr2, pad], axis=0)
        k_ref[0, hd] = kT.T.astype(BF16)
        rows = _value_rows(kvT[nope_w + hd * C_V:nope_w + (hd + 1) * C_V])
        for c in range(TM // TK_C):
            v_ref[0, hd, c] = rows[:, c * TK_C:(c + 1) * TK_C]


def _proj_c(xb, w_c, gq, gkv, w_uq, w_ukv, c1, s1, B, S):
    nt = S // TM
    tab = pl.BlockSpec((ROPE_FREQS, TM), lambda b, i: (0, i))
    full = lambda a: pl.BlockSpec(a.shape, lambda b, i: (0,) * a.ndim)
    gq2, gkv2 = gq.reshape(1, -1), gkv.reshape(1, -1)
    return pl.pallas_call(
        _proj_c_kernel,
        grid=(B, nt),
        in_specs=[
            pl.BlockSpec((TM, D_MODEL), lambda b, i: (b * nt + i, 0)),
            full(w_c), full(gq2), full(gkv2), full(w_uq), full(w_ukv), tab, tab,
        ],
        out_specs=[
            pl.BlockSpec((1, C_HEADS, C_QK, TM), lambda b, i: (b, 0, 0, i)),
            pl.BlockSpec((1, C_HEADS, TM, C_QK), lambda b, i: (b, 0, i, 0)),
            pl.BlockSpec((1, C_HEADS, TM // TK_C, V_ROWS, TK_C), lambda b, i: (b, 0, i, 0, 0)),
        ],
        out_shape=[
            jax.ShapeDtypeStruct((B, C_HEADS, C_QK, S), BF16),
            jax.ShapeDtypeStruct((B, C_HEADS, S, C_QK), BF16),
            jax.ShapeDtypeStruct((B, C_HEADS, S // TK_C, V_ROWS, TK_C), BF16),
        ],
        compiler_params=_params(("parallel", "parallel")),
        name="proj_c",
    )(xb, w_c, gq2, gkv2, w_uq, w_ukv, c1, s1)


def _dense_attn_kernel(q_ref, k_ref, v_ref, o_ref, s_ref, smax_ref, m_ref, acc_ref, *, group, tq, tk, n_chunks,
                       unroll):
    q = jnp.concatenate([q_ref[0, g] for g in range(group)], axis=1) if group > 1 else q_ref[0, 0]
    m_ref[...] = jnp.full(m_ref.shape, NEG_INF, F32)
    acc_ref[...] = jnp.zeros(acc_ref.shape, F32)

    def fill(slot, c):
        kc = k_ref[0, 0, pl.ds(pl.multiple_of(c * tk, tk), tk), :]
        s = _dot(kc, q)
        s_ref[slot] = s
        smax_ref[slot] = jnp.max(s, axis=0, keepdims=True)

    def consume(slot, c):
        m_old = m_ref[...]
        m_new = jnp.maximum(m_old, smax_ref[slot])
        p = jnp.exp2(s_ref[slot] - m_new).astype(BF16)
        vc = v_ref[0, 0, c]
        if group > 1:
            pv = jnp.concatenate([_dot(vc, p[:, g * tq:(g + 1) * tq]) for g in range(group)], axis=1)
        else:
            pv = _dot(vc, p)
        acc_ref[...] = acc_ref[...] * jnp.exp2(m_old - m_new) + pv
        m_ref[...] = m_new

    fill(0, 0)

    def group_of_chunks(i, carry):
        c = unroll * i
        for u in range(unroll):
            fill((u + 1) % 2, c + u + 1)
            consume(u % 2, c + u)
        return carry

    lax.fori_loop(0, n_chunks // unroll - 1, group_of_chunks, 0)
    c = n_chunks - unroll
    for u in range(unroll - 1):
        fill((u + 1) % 2, c + u + 1)
        consume(u % 2, c + u)
    consume((unroll - 1) % 2, n_chunks - 1)
    acc = acc_ref[...]
    o = acc[0:HEAD_DIM] / acc[HEAD_DIM:HEAD_DIM + 1]
    for g in range(group):
        o_ref[0, g * HEAD_DIM:(g + 1) * HEAD_DIM, :] = o[:, g * tq:(g + 1) * tq].astype(BF16)


def _mixer_a_tiling(n_chunks):
    return A_GROUP, TQ_A, 8 if n_chunks > 8 else 4


def _dense_attn(qT, k, vT, *, group, tq, tk, unroll, name):
    B, hq, dk, S = qT.shape
    hkv_arrays = k.shape[1]
    n_kv = hq // group
    v_share = n_kv // vT.shape[1]
    n_chunks = S // tk
    if n_chunks <= unroll:
        unroll = 2
    assert n_chunks % unroll == 0 and unroll >= 2
    kernel = functools.partial(_dense_attn_kernel, group=group, tq=tq, tk=tk, n_chunks=n_chunks, unroll=unroll)
    k_map = (lambda b, h, i: (b, h, 0, 0)) if hkv_arrays > 1 else (lambda b, h, i: (b, 0, 0, 0))
    return pl.pallas_call(
        kernel,
        grid=(B, n_kv, S // tq),
        in_specs=[
            pl.BlockSpec((1, group, dk, tq), lambda b, h, i: (b, h, 0, i)),
            pl.BlockSpec((1, 1, S, dk), k_map),
            pl.BlockSpec((1, 1, n_chunks, V_ROWS, tk), lambda b, h, i: (b, h // v_share, 0, 0, 0)),
        ],
        out_specs=pl.BlockSpec((1, group * HEAD_DIM, tq), lambda b, h, i: (b, h, i)),
        out_shape=jax.ShapeDtypeStruct((B, hq * HEAD_DIM, S), BF16),
        scratch_shapes=[pltpu.VMEM((2, tk, group * tq), F32), pltpu.VMEM((2, 1, group * tq), F32),
                        pltpu.VMEM((1, group * tq), F32),
                        pltpu.VMEM((V_ROWS, group * tq), F32)],
        compiler_params=_params(("parallel", "parallel", "parallel")),
        name=name,
    )(qT, k, vT)


def _alibi_window_table():
    n_off = 4 * B_REACH + 1
    off = (jnp.arange(n_off, dtype=jnp.int32) - 2 * B_REACH)[:, None, None] * BQ
    kk = jnp.arange(BQ, dtype=jnp.int32)[None, :, None]
    qq = jnp.arange(BQ, dtype=jnp.int32)[None, None, :]
    dist = jnp.abs(off + kk - qq)
    mult = sum(((dist % dilation == 0) & (dist <= window // 2)).astype(F32)
               for window, dilation in zip(B_WINDOWS, B_DILATIONS))
    slopes = jnp.exp2(-ALIBI_MAX_BIAS * jnp.arange(1, B_HEADS + 1, dtype=F32) / B_HEADS)
    bias = -slopes[:, None, None, None] * dist.astype(F32)[None] + jnp.log(jnp.maximum(mult, 1.0))[None]
    tbl = jnp.where(mult[None] > 0, bias * LOG2E, NEG_INF)
    return tbl.reshape(B_HEADS, n_off * BQ, BQ)


def _window_attn_kernel(q_ref, k_ref, v_ref, t_ref, o_ref, s_ref, smax_ref, *, n_chunks):
    step = pl.program_id(2)

    def window(qq):
        qc = step * B_QCHUNKS + qq
        w0 = jnp.clip(qc - B_REACH, 0, n_chunks - B_WINDOW)
        return w0, pl.multiple_of((w0 - qc + 2 * B_REACH) * BQ, BQ)

    def fill(slot, hd, qq):
        w0, t0 = window(qq)
        kw = k_ref[0, 0, pl.ds(pl.multiple_of(w0 * BQ, BQ), B_WINDOW * BQ), :]
        s = _dot(kw, q_ref[0, hd, :, qq * BQ:(qq + 1) * BQ]) + t_ref[hd, pl.ds(t0, B_WINDOW * BQ), :]
        s_ref[slot] = s
        smax_ref[slot] = jnp.max(s, axis=0, keepdims=True)

    def consume(slot, hd, qq):
        w0, _ = window(qq)
        p = jnp.exp2(s_ref[slot] - smax_ref[slot]).astype(BF16)
        acc = _dot(v_ref[0, hd, w0], p[0:BQ])
        for u in range(1, B_WINDOW):
            acc = acc + _dot(v_ref[0, hd, w0 + u], p[u * BQ:(u + 1) * BQ])
        o_ref[0, hd * HEAD_DIM:(hd + 1) * HEAD_DIM, qq * BQ:(qq + 1) * BQ] = (
            acc[0:HEAD_DIM] / acc[HEAD_DIM:HEAD_DIM + 1]).astype(BF16)

    chains = [(hd, qq) for qq in range(B_QCHUNKS) for hd in range(2)]
    fill(0, *chains[0])
    for n, chain in enumerate(chains):
        if n + 1 < len(chains):
            fill((n + 1) % 2, *chains[n + 1])
        consume(n % 2, *chain)


def _window_attn(qT, k, vT, table):
    B, _, _, S = qT.shape
    n_chunks = S // BQ
    assert n_chunks >= B_WINDOW and n_chunks % B_QCHUNKS == 0
    kernel = functools.partial(_window_attn_kernel, n_chunks=n_chunks)
    return pl.pallas_call(
        kernel,
        grid=(B, B_HEADS // 2, n_chunks // B_QCHUNKS),
        in_specs=[
            pl.BlockSpec((1, 2, 2 * HEAD_DIM, B_QCHUNKS * BQ), lambda b, h, i: (b, h, 0, i)),
            pl.BlockSpec((1, 1, S, 2 * HEAD_DIM), lambda b, h, i: (b, h, 0, 0)),
            pl.BlockSpec((1, 2, n_chunks, V_ROWS, BQ), lambda b, h, i: (b, h, 0, 0, 0)),
            pl.BlockSpec((2, (4 * B_REACH + 1) * BQ, BQ), lambda b, h, i: (h, 0, 0)),
        ],
        out_specs=pl.BlockSpec((1, 2 * HEAD_DIM, B_QCHUNKS * BQ), lambda b, h, i: (b, h, i)),
        out_shape=jax.ShapeDtypeStruct((B, B_W, S), BF16),
        scratch_shapes=[pltpu.VMEM((2, B_WINDOW * BQ, BQ), F32), pltpu.VMEM((2, 1, BQ), F32)],
        compiler_params=_params(("parallel", "parallel", "parallel")),
        name="window_attn",
    )(qT, k, vT, table)


def _out_proj_kernel(a_ref, b_ref, c_ref, w_ref, x_ref, g_ref, beta_ref, of_ref, ob_ref):
    mixT = jnp.concatenate([a_ref[0], b_ref[0], c_ref[0]], axis=0)
    mix = mixT.astype(F32).T.astype(BF16)
    y = _dot(mix, w_ref[...])
    x1 = _layer_norm(ALPHA * x_ref[...] + y, g_ref[...], beta_ref[...])
    of_ref[...] = x1
    ob_ref[...] = x1.astype(BF16)


def _out_proj(aT, bT, cT, w_o, x, g, beta, B, S):
    nt = S // TM
    n = B * S
    row = pl.BlockSpec((TM, D_MODEL), lambda b, i: (b * nt + i, 0))
    vec = pl.BlockSpec((1, D_MODEL), lambda b, i: (0, 0))
    featT = lambda a: pl.BlockSpec((1, a.shape[1], TM), lambda b, i: (b, 0, i))
    return pl.pallas_call(
        _out_proj_kernel,
        grid=(B, nt),
        in_specs=[featT(aT), featT(bT), featT(cT), pl.BlockSpec(w_o.shape, lambda b, i: (0, 0)), row, vec, vec],
        out_specs=[row, row],
        out_shape=[jax.ShapeDtypeStruct((n, D_MODEL), F32), jax.ShapeDtypeStruct((n, D_MODEL), BF16)],
        compiler_params=_params(("parallel", "parallel")),
        name="out_proj",
    )(aT, bT, cT, w_o, x, g.reshape(1, -1), beta.reshape(1, -1))


HALO = 16


def _ffn_up_kernel(xb_ref, prev_ref, next_ref, wg_ref, wv_ref, wc_ref, bc_ref, h_ref, *, n_tiles):
    i = pl.program_id(1)
    xb = xb_ref[...]
    wg = wg_ref[...]
    gate = _dot(xb, wg)
    val = _dot(xb, wv_ref[...])
    g_prev = _dot(prev_ref[...], wg)[HALO - 1:HALO]
    g_next = _dot(next_ref[...], wg)[0:1]
    g_prev = jnp.where(i > 0, g_prev, 0.0)
    g_next = jnp.where(i < n_tiles - 1, g_next, 0.0)
    row = lax.broadcasted_iota(jnp.int32, gate.shape, 0)
    up = jnp.where(row == 0, g_prev, pltpu.roll(gate, 1, axis=0))
    down = jnp.where(row == TM_UP - 1, g_next, pltpu.roll(gate, TM_UP - 1, axis=0))
    wc = wc_ref[...]
    conv = up * wc[0:1] + gate * wc[1:2] + down * wc[2:3] + bc_ref[...]
    act = 0.5 * conv * (1.0 + lax.erf(conv * (2.0 ** -0.5)))
    h_ref[...] = (act * val).astype(BF16)


def _ffn_down_kernel(h_ref, wd_ref, xf_ref, g_ref, beta_ref, of_ref, ob_ref):
    y = _dot(h_ref[...], wd_ref[...])
    x2 = _layer_norm(ALPHA * xf_ref[...] + y, g_ref[...], beta_ref[...])
    of_ref[...] = x2
    ob_ref[...] = x2.astype(BF16)


def _ffn(xb, xf, w_gate, w_val, w_conv, b_conv, w_down, g, beta, B, S):
    n = B * S
    nt = S // TM_UP
    hb = TM_UP // HALO
    last_halo = n // HALO - 1
    hid = pl.pallas_call(
        functools.partial(_ffn_up_kernel, n_tiles=nt),
        grid=(B, nt, D_FF // TF),
        in_specs=[
            pl.BlockSpec((TM_UP, D_MODEL), lambda b, i, j: (b * nt + i, 0)),
            pl.BlockSpec((HALO, D_MODEL), lambda b, i, j: (jnp.maximum((b * nt + i) * hb - 1, 0), 0)),
            pl.BlockSpec((HALO, D_MODEL), lambda b, i, j: (jnp.minimum((b * nt + i + 1) * hb, last_halo), 0)),
            pl.BlockSpec((D_MODEL, TF), lambda b, i, j: (0, j)),
            pl.BlockSpec((D_MODEL, TF), lambda b, i, j: (0, j)),
            pl.BlockSpec((CONV_W, TF), lambda b, i, j: (0, j)),
            pl.BlockSpec((1, TF), lambda b, i, j: (0, j)),
        ],
        out_specs=pl.BlockSpec((TM_UP, TF), lambda b, i, j: (b * nt + i, j)),
        out_shape=jax.ShapeDtypeStruct((n, D_FF), BF16),
        compiler_params=_params(("parallel", "parallel", "parallel")),
        name="ffn_up",
    )(xb, xb, xb, w_gate, w_val, w_conv, b_conv.reshape(1, -1))
    row = pl.BlockSpec((TM_DOWN, D_MODEL), lambda i: (i, 0))
    vec = pl.BlockSpec((1, D_MODEL), lambda i: (0, 0))
    return pl.pallas_call(
        _ffn_down_kernel,
        grid=(n // TM_DOWN,),
        in_specs=[
            pl.BlockSpec((TM_DOWN, D_FF), lambda i: (i, 0)),
            pl.BlockSpec((D_FF, D_MODEL), lambda i: (0, 0), pipeline_mode=pl.Buffered(1)),
            row, vec, vec,
        ],
        out_specs=[row, row],
        out_shape=[jax.ShapeDtypeStruct((n, D_MODEL), F32), jax.ShapeDtypeStruct((n, D_MODEL), BF16)],
        compiler_params=_params(("parallel",)),
        name="ffn_down",
    )(hid, w_down, xf, g.reshape(1, -1), beta.reshape(1, -1))


def _rope_tables(pos, dim):
    inv_freq = jnp.power(ROPE_THETA, -jnp.arange(0, dim, 2, dtype=F32) / dim)
    ang = pos.astype(F32)[:, None] * inv_freq[None, :]
    return jnp.cos(ang).T, jnp.sin(ang).T


def _position_tables(S):
    rows = S // GRID_W
    row = jnp.repeat(jnp.arange(rows, dtype=jnp.int32), GRID_W)
    col = jnp.tile(jnp.arange(GRID_W, dtype=jnp.int32), rows)
    t = jnp.arange(S, dtype=jnp.int32)
    cr, sr = _rope_tables(row, HEAD_DIM // 2)
    cc, sc = _rope_tables(col, HEAD_DIM // 2)
    c1, s1 = _rope_tables(t, C_ROPE)
    return (cr, sr, cc, sc), (c1, s1)


def _prepare_weights(w_in, w_uq, w_ukv, w_o, w_up, w_down):
    o = np.cumsum([0, A_W, A_KV_W, A_KV_W, B_W, B_W, B_W, C_Q_LORA, C_KV_LORA, C_ROPE])
    layers = []
    uq_cols = np.concatenate(
        [np.concatenate([np.arange(h * 96, h * 96 + C_NOPE) for h in range(C_HEADS)]),
         np.concatenate([np.arange(h * 96 + C_NOPE, (h + 1) * 96) for h in range(C_HEADS)])])
    ukv_cols = np.concatenate(
        [np.concatenate([np.arange(h * 128, h * 128 + C_NOPE) for h in range(C_HEADS)]),
         np.concatenate([np.arange(h * 128 + C_NOPE, (h + 1) * 128) for h in range(C_HEADS)])])
    for l in range(DEPTH):
        wi = w_in[l].astype(BF16)
        w_c = jnp.pad(wi[:, o[6]:o[9]], ((0, 0), (0, 128 - C_ROPE)))
        layers.append(dict(
            w_a=wi[:, o[0]:o[3]], w_b=wi[:, o[3]:o[6]], w_c=w_c,
            w_uq=w_uq[l].astype(BF16)[:, uq_cols], w_ukv=w_ukv[l].astype(BF16)[:, ukv_cols],
            w_o=w_o[l].astype(BF16),
            w_gate=w_up[l, :, :D_FF].astype(BF16), w_val=w_up[l, :, D_FF:].astype(BF16),
            w_down=w_down[l].astype(BF16)))
    return layers


def _trunk(x, weights, table, ln_emb_g, ln_emb_b, a_q_norm, a_k_norm, c_q_norm, c_kv_norm,
           ln1_g, ln1_b, w_conv, b_conv, ln2_g, ln2_b):
    B, S, _ = x.shape
    tabs_a, (c1, s1) = _position_tables(S)
    xf, xb = _embed_ln(x.reshape(B * S, D_MODEL), ln_emb_g, ln_emb_b)
    for l in range(DEPTH):
        w = weights[l]
        aq, ak, av = _proj_a(xb, w["w_a"], a_q_norm[l], a_k_norm[l], tabs_a, B, S)
        bq, bk, bv = _proj_b(xb, w["w_b"], B, S)
        cq, ck, cv = _proj_c(xb, w["w_c"], c_q_norm[l], c_kv_norm[l], w["w_uq"], w["w_ukv"], c1, s1, B, S)
        a_group, a_tq, a_unroll = _mixer_a_tiling(S // TK_A)
        a_out = _dense_attn(aq, ak.reshape(B, 1, S, A_KV_W), av, group=a_group, tq=a_tq, tk=TK_A,
                            unroll=a_unroll, name="attn_a")
        b_out = _window_attn(bq, bk, bv, table)
        c_out = _dense_attn(cq, ck, cv, group=1, tq=TQ_C, tk=TK_C, unroll=CHUNKS_PER_TRIP_C, name="attn_c")
        xf, xb = _out_proj(a_out, b_out, c_out, w["w_o"], xf, ln1_g[l], ln1_b[l], B, S)
        xf, xb = _ffn(xb, xf, w["w_gate"], w["w_val"], w_conv[l], b_conv[l], w["w_down"], ln2_g[l], ln2_b[l], B, S)
    return xf.reshape(B, S, D_MODEL)


def kernel(x_prompt, x_sample, ln_emb_g, ln_emb_b, w_in, a_q_norm, a_k_norm, c_q_norm, c_kv_norm, w_uq, w_ukv,
           w_o, ln1_g, ln1_b, w_up, w_conv, b_conv, w_down, ln2_g, ln2_b):
    weights = _prepare_weights(w_in, w_uq, w_ukv, w_o, w_up, w_down)
    table = _alibi_window_table()
    rest = (ln_emb_g, ln_emb_b, a_q_norm, a_k_norm, c_q_norm, c_kv_norm, ln1_g, ln1_b, w_conv, b_conv, ln2_g, ln2_b)
    return (_trunk(x_prompt, weights, table, *rest), _trunk(x_sample, weights, table, *rest))
```

```python
import functools
import math

import numpy as np
import jax
import jax.numpy as jnp
from jax import lax
from jax.experimental import pallas as pl
from jax.experimental.pallas import tpu as pltpu

F32 = jnp.float32
BF16 = jnp.bfloat16

D_MODEL = 2048
DEPTH = 2
HEAD_DIM = 64
GRID_W = 64
ROPE_THETA = 10000.0
NORM_EPS = 1e-6
NEG_INF = -1e30
A_Q_HEADS = 12
A_KV_HEADS = 4
A_GROUP = A_Q_HEADS // A_KV_HEADS
B_HEADS = 12
B_WINDOWS = (128, 512, 2048)
B_DILATIONS = (1, 4, 16)
ALIBI_MAX_BIAS = 8.0
C_HEADS = 8
C_NOPE = 64
C_ROPE = 32
C_V = 64
C_Q_LORA = 512
C_KV_LORA = 256
A_W = A_Q_HEADS * HEAD_DIM
A_KV_W = A_KV_HEADS * HEAD_DIM
B_W = B_HEADS * HEAD_DIM
C_W = C_HEADS * C_V
D_FF = 5632
CONV_W = 3
ALPHA = (2.0 * DEPTH) ** 0.25

ROPE_FREQS = 16
V_ROWS = HEAD_DIM + 16
C_QK = 128

VMEM_LIMIT_BYTES = 56 * 1024 * 1024

TM = 512
EPILOGUE_ROWS = 128
TK_A = 512
TK_C = 512
TQ_A = 512
CHUNKS_PER_TRIP_C = 8
TQ_C = 1024
BQ = 256
B_REACH = 4
B_WINDOW = 2 * B_REACH + 1
B_QCHUNKS = 8
LOG2E = math.log2(math.e)
TF = 512
TM_UP = 1024
TM_DOWN = 256


def _params(sem):
    return pltpu.CompilerParams(dimension_semantics=sem, vmem_limit_bytes=VMEM_LIMIT_BYTES)


def _dot(a, b):
    return jnp.dot(a, b, preferred_element_type=F32)


def _layer_norm(y, g, b):
    mu = jnp.mean(y, axis=-1, keepdims=True)
    yc = y - mu
    var = jnp.mean(yc * yc, axis=-1, keepdims=True)
    return yc * lax.rsqrt(var + NORM_EPS) * g + b


def _value_rows(vT):
    n = vT.shape[1]
    row = lax.broadcasted_iota(jnp.int32, (V_ROWS - HEAD_DIM, n), 0)
    ones = jnp.where(row == 0, 1.0, 0.0).astype(F32)
    return jnp.concatenate([vT, ones], axis=0).astype(BF16)


def _rope_rows(x1, x2, c, s):
    return x1 * c - x2 * s, x2 * c + x1 * s


def _ln_kernel(x_ref, g_ref, b_ref, of_ref, ob_ref):
    y = _layer_norm(x_ref[...], g_ref[...], b_ref[...])
    of_ref[...] = y
    ob_ref[...] = y.astype(BF16)


def _embed_ln(x, g, b):
    n = x.shape[0]
    row = pl.BlockSpec((TM, D_MODEL), lambda i: (i, 0))
    vec = pl.BlockSpec((1, D_MODEL), lambda i: (0, 0))
    return pl.pallas_call(
        _ln_kernel,
        grid=(n // TM,),
        in_specs=[row, vec, vec],
        out_specs=[row, row],
        out_shape=[jax.ShapeDtypeStruct((n, D_MODEL), F32), jax.ShapeDtypeStruct((n, D_MODEL), BF16)],
        compiler_params=_params(("parallel",)),
        name="embed_ln",
    )(x, g.reshape(1, -1), b.reshape(1, -1))


def _proj_a_kernel(x_ref, w_ref, gq_ref, gk_ref, cr_ref, sr_ref, cc_ref, sc_ref, q_ref, k_ref, v_ref):
    h = _dot(x_ref[...], w_ref[...])
    hT = h.T
    cr, sr, cc, sc = cr_ref[...], sr_ref[...], cc_ref[...], sc_ref[...]

    def norm_rope(blk, g):
        ms = jnp.mean(blk * blk, axis=0, keepdims=True)
        y = blk * lax.rsqrt(ms + NORM_EPS) * g
        r1, r2 = _rope_rows(y[0:16], y[16:32], cr, sr)
        r3, r4 = _rope_rows(y[32:48], y[48:64], cc, sc)
        return jnp.concatenate([r1, r2, r3, r4], axis=0)

    gq, gk = gq_ref[...], gk_ref[...]
    zeros = jnp.zeros((A_KV_W, TM), BF16)
    for hq in range(A_Q_HEADS):
        kvh = hq // A_GROUP
        qh = (norm_rope(hT[hq * HEAD_DIM:(hq + 1) * HEAD_DIM], gq) * (HEAD_DIM ** -0.5 * LOG2E)).astype(BF16)
        q_ref[0, hq] = zeros
        q_ref[0, hq, kvh * HEAD_DIM:(kvh + 1) * HEAD_DIM, :] = qh
    kT = jnp.concatenate(
        [norm_rope(hT[A_W + j * HEAD_DIM:A_W + (j + 1) * HEAD_DIM], gk) for j in range(A_KV_HEADS)], axis=0)
    k_ref[0] = kT.T.astype(BF16)
    v0 = A_W + A_KV_W
    for j in range(A_KV_HEADS):
        rows = _value_rows(hT[v0 + j * HEAD_DIM:v0 + (j + 1) * HEAD_DIM])
        for c in range(TM // TK_A):
            v_ref[0, j, c] = rows[:, c * TK_A:(c + 1) * TK_A]


def _proj_a(xb, w_a, gq, gk, tabs, B, S):
    cr, sr, cc, sc = tabs
    nt = S // TM
    tab = pl.BlockSpec((ROPE_FREQS, TM), lambda b, i: (0, i))
    gain = pl.BlockSpec((HEAD_DIM, 1), lambda b, i: (0, 0))
    return pl.pallas_call(
        _proj_a_kernel,
        grid=(B, nt),
        in_specs=[
            pl.BlockSpec((TM, D_MODEL), lambda b, i: (b * nt + i, 0)),
            pl.BlockSpec(w_a.shape, lambda b, i: (0, 0)),
            gain, gain, tab, tab, tab, tab,
        ],
        out_specs=[
            pl.BlockSpec((1, A_Q_HEADS, A_KV_W, TM), lambda b, i: (b, 0, 0, i)),
            pl.BlockSpec((1, TM, A_KV_W), lambda b, i: (b, i, 0)),
            pl.BlockSpec((1, A_KV_HEADS, TM // TK_A, V_ROWS, TK_A), lambda b, i: (b, 0, i, 0, 0)),
        ],
        out_shape=[
            jax.ShapeDtypeStruct((B, A_Q_HEADS, A_KV_W, S), BF16),
            jax.ShapeDtypeStruct((B, S, A_KV_W), BF16),
            jax.ShapeDtypeStruct((B, A_KV_HEADS, S // TK_A, V_ROWS, TK_A), BF16),
        ],
        compiler_params=_params(("parallel", "parallel")),
        name="proj_a",
    )(xb, w_a, gq.reshape(-1, 1), gk.reshape(-1, 1), cr, sr, cc, sc)


def _proj_b_kernel(x_ref, w_ref, q_ref, k_ref, v_ref):
    h = _dot(x_ref[...], w_ref[...])
    hq = h[:, 0:B_W]
    qT = (hq * (HEAD_DIM ** -0.5 * LOG2E)).T.astype(BF16)
    zeros = jnp.zeros((HEAD_DIM, TM), BF16)
    for hd in range(B_HEADS):
        lo = (hd % 2) * HEAD_DIM
        q_ref[0, hd, lo:lo + HEAD_DIM, :] = qT[hd * HEAD_DIM:(hd + 1) * HEAD_DIM]
        q_ref[0, hd, HEAD_DIM - lo:2 * HEAD_DIM - lo, :] = zeros
    for p in range(B_HEADS // 2):
        k_ref[0, p] = h[:, B_W + p * 128:B_W + (p + 1) * 128].astype(BF16)
    vT = h[:, 2 * B_W:3 * B_W].T
    for hd in range(B_HEADS):
        rows = _value_rows(vT[hd * HEAD_DIM:(hd + 1) * HEAD_DIM])
        for c in range(TM // BQ):
            v_ref[0, hd, c] = rows[:, c * BQ:(c + 1) * BQ]


def _proj_b(xb, w_b, B, S):
    nt = S // TM
    return pl.pallas_call(
        _proj_b_kernel,
        grid=(B, nt),
        in_specs=[
            pl.BlockSpec((TM, D_MODEL), lambda b, i: (b * nt + i, 0)),
            pl.BlockSpec(w_b.shape, lambda b, i: (0, 0)),
        ],
        out_specs=[
            pl.BlockSpec((1, B_HEADS, 2 * HEAD_DIM, TM), lambda b, i: (b, 0, 0, i)),
            pl.BlockSpec((1, B_HEADS // 2, TM, 2 * HEAD_DIM), lambda b, i: (b, 0, i, 0)),
            pl.BlockSpec((1, B_HEADS, TM // BQ, V_ROWS, BQ), lambda b, i: (b, 0, i, 0, 0)),
        ],
        out_shape=[
            jax.ShapeDtypeStruct((B, B_HEADS, 2 * HEAD_DIM, S), BF16),
            jax.ShapeDtypeStruct((B, B_HEADS // 2, S, 2 * HEAD_DIM), BF16),
            jax.ShapeDtypeStruct((B, B_HEADS, S // BQ, V_ROWS, BQ), BF16),
        ],
        compiler_params=_params(("parallel", "parallel")),
        name="proj_b",
    )(xb, w_b)


def _proj_c_kernel(x_ref, w_ref, gq_ref, gkv_ref, wuq_ref, wukv_ref, c1_ref, s1_ref, q_ref, k_ref, v_ref):
    h = _dot(x_ref[...], w_ref[...])
    c1, s1 = c1_ref[...], s1_ref[...]

    def rms(x, g):
        ms = jnp.mean(x * x, axis=-1, keepdims=True)
        return (x * lax.rsqrt(ms + NORM_EPS) * g).astype(BF16)

    cq = rms(h[:, 0:C_Q_LORA], gq_ref[...])
    ckv = rms(h[:, C_Q_LORA:C_Q_LORA + C_KV_LORA], gkv_ref[...])
    qcT = _dot(cq, wuq_ref[...]).T * ((C_NOPE + C_ROPE) ** -0.5 * LOG2E)
    kvT = _dot(ckv, wukv_ref[...]).T
    krT = h[:, C_Q_LORA + C_KV_LORA:].T
    kr1, kr2 = _rope_rows(krT[0:16], krT[16:32], c1, s1)
    pad = jnp.zeros((C_QK - C_NOPE - C_ROPE, TM), F32)
    nope_w = C_HEADS * C_NOPE
    for hd in range(C_HEADS):
        r0 = nope_w + hd * C_ROPE
        qr1, qr2 = _rope_rows(qcT[r0:r0 + 16], qcT[r0 + 16:r0 + 32], c1, s1)
        q_ref[0, hd] = jnp.concatenate(
            [qcT[hd * C_NOPE:(hd + 1) * C_NOPE], qr1, qr2, pad], axis=0).astype(BF16)
        kT = jnp.concatenate([kvT[hd * C_NOPE:(hd + 1) * C_NOPE], kr1, kr2, pad], axis=0)
        k_ref[0, hd] = kT.T.astype(BF16)
        rows = _value_rows(kvT[nope_w + hd * C_V:nope_w + (hd + 1) * C_V])
        for c in range(TM // TK_C):
            v_ref[0, hd, c] = rows[:, c * TK_C:(c + 1) * TK_C]


def _proj_c(xb, w_c, gq, gkv, w_uq, w_ukv, c1, s1, B, S):
    nt = S // TM
    tab = pl.BlockSpec((ROPE_FREQS, TM), lambda b, i: (0, i))
    full = lambda a: pl.BlockSpec(a.shape, lambda b, i: (0,) * a.ndim)
    gq2, gkv2 = gq.reshape(1, -1), gkv.reshape(1, -1)
    return pl.pallas_call(
        _proj_c_kernel,
        grid=(B, nt),
        in_specs=[
            pl.BlockSpec((TM, D_MODEL), lambda b, i: (b * nt + i, 0)),
            full(w_c), full(gq2), full(gkv2), full(w_uq), full(w_ukv), tab, tab,
        ],
        out_specs=[
            pl.BlockSpec((1, C_HEADS, C_QK, TM), lambda b, i: (b, 0, 0, i)),
            pl.BlockSpec((1, C_HEADS, TM, C_QK), lambda b, i: (b, 0, i, 0)),
            pl.BlockSpec((1, C_HEADS, TM // TK_C, V_ROWS, TK_C), lambda b, i: (b, 0, i, 0, 0)),
        ],
        out_shape=[
            jax.ShapeDtypeStruct((B, C_HEADS, C_QK, S), BF16),
            jax.ShapeDtypeStruct((B, C_HEADS, S, C_QK), BF16),
            jax.ShapeDtypeStruct((B, C_HEADS, S // TK_C, V_ROWS, TK_C), BF16),
        ],
        compiler_params=_params(("parallel", "parallel")),
        name="proj_c",
    )(xb, w_c, gq2, gkv2, w_uq, w_ukv, c1, s1)


def _dense_attn_kernel(q_ref, k_ref, v_ref, o_ref, s_ref, smax_ref, m_ref, acc_ref, *, group, tq, tk, n_chunks,
                       unroll):
    q = jnp.concatenate([q_ref[0, g] for g in range(group)], axis=1) if group > 1 else q_ref[0, 0]
    m_ref[...] = jnp.full(m_ref.shape, NEG_INF, F32)
    acc_ref[...] = jnp.zeros(acc_ref.shape, F32)

    def fill(slot, c):
        kc = k_ref[0, 0, pl.ds(pl.multiple_of(c * tk, tk), tk), :]
        s = _dot(kc, q)
        s_ref[slot] = s
        smax_ref[slot] = jnp.max(s, axis=0, keepdims=True)

    def consume(slot, c):
        m_old = m_ref[...]
        m_new = jnp.maximum(m_old, smax_ref[slot])
        p = jnp.exp2(s_ref[slot] - m_new).astype(BF16)
        vc = v_ref[0, 0, c]
        if group > 1:
            pv = jnp.concatenate([_dot(vc, p[:, g * tq:(g + 1) * tq]) for g in range(group)], axis=1)
        else:
            pv = _dot(vc, p)
        acc_ref[...] = acc_ref[...] * jnp.exp2(m_old - m_new) + pv
        m_ref[...] = m_new

    fill(0, 0)

    def group_of_chunks(i, carry):
        c = unroll * i
        for u in range(unroll):
            fill((u + 1) % 2, c + u + 1)
            consume(u % 2, c + u)
        return carry

    lax.fori_loop(0, n_chunks // unroll - 1, group_of_chunks, 0)
    c = n_chunks - unroll
    for u in range(unroll - 1):
        fill((u + 1) % 2, c + u + 1)
        consume(u % 2, c + u)
    consume((unroll - 1) % 2, n_chunks - 1)
    acc = acc_ref[...]
    o = acc[0:HEAD_DIM] / acc[HEAD_DIM:HEAD_DIM + 1]
    for g in range(group):
        o_ref[0, g * HEAD_DIM:(g + 1) * HEAD_DIM, :] = o[:, g * tq:(g + 1) * tq].astype(BF16)


def _mixer_a_tiling(n_chunks):
    return A_GROUP, TQ_A, 16 if n_chunks > 16 else 4


def _dense_attn(qT, k, vT, *, group, tq, tk, unroll, name):
    B, hq, dk, S = qT.shape
    hkv_arrays = k.shape[1]
    n_kv = hq // group
    v_share = n_kv // vT.shape[1]
    n_chunks = S // tk
    if n_chunks <= unroll:
        unroll = 2
    assert n_chunks % unroll == 0 and unroll >= 2
    kernel = functools.partial(_dense_attn_kernel, group=group, tq=tq, tk=tk, n_chunks=n_chunks, unroll=unroll)
    k_map = (lambda b, h, i: (b, h, 0, 0)) if hkv_arrays > 1 else (lambda b, h, i: (b, 0, 0, 0))
    return pl.pallas_call(
        kernel,
        grid=(B, n_kv, S // tq),
        in_specs=[
            pl.BlockSpec((1, group, dk, tq), lambda b, h, i: (b, h, 0, i)),
            pl.BlockSpec((1, 1, S, dk), k_map),
            pl.BlockSpec((1, 1, n_chunks, V_ROWS, tk), lambda b, h, i: (b, h // v_share, 0, 0, 0)),
        ],
        out_specs=pl.BlockSpec((1, group * HEAD_DIM, tq), lambda b, h, i: (b, h, i)),
        out_shape=jax.ShapeDtypeStruct((B, hq * HEAD_DIM, S), BF16),
        scratch_shapes=[pltpu.VMEM((2, tk, group * tq), F32), pltpu.VMEM((2, 1, group * tq), F32),
                        pltpu.VMEM((1, group * tq), F32),
                        pltpu.VMEM((V_ROWS, group * tq), F32)],
        compiler_params=_params(("parallel", "parallel", "parallel")),
        name=name,
    )(qT, k, vT)


def _alibi_window_table():
    n_off = 4 * B_REACH + 1
    off = (jnp.arange(n_off, dtype=jnp.int32) - 2 * B_REACH)[:, None, None] * BQ
    kk = jnp.arange(BQ, dtype=jnp.int32)[None, :, None]
    qq = jnp.arange(BQ, dtype=jnp.int32)[None, None, :]
    dist = jnp.abs(off + kk - qq)
    mult = sum(((dist % dilation == 0) & (dist <= window // 2)).astype(F32)
               for window, dilation in zip(B_WINDOWS, B_DILATIONS))
    slopes = jnp.exp2(-ALIBI_MAX_BIAS * jnp.arange(1, B_HEADS + 1, dtype=F32) / B_HEADS)
    bias = -slopes[:, None, None, None] * dist.astype(F32)[None] + jnp.log(jnp.maximum(mult, 1.0))[None]
    tbl = jnp.where(mult[None] > 0, bias * LOG2E, NEG_INF)
    return tbl.reshape(B_HEADS, n_off * BQ, BQ)


def _window_attn_kernel(q_ref, k_ref, v_ref, t_ref, o_ref, s_ref, smax_ref, *, n_chunks):
    step = pl.program_id(2)

    def window(qq):
        qc = step * B_QCHUNKS + qq
        w0 = jnp.clip(qc - B_REACH, 0, n_chunks - B_WINDOW)
        return w0, pl.multiple_of((w0 - qc + 2 * B_REACH) * BQ, BQ)

    def fill(slot, hd, qq):
        w0, t0 = window(qq)
        kw = k_ref[0, 0, pl.ds(pl.multiple_of(w0 * BQ, BQ), B_WINDOW * BQ), :]
        s = _dot(kw, q_ref[0, hd, :, qq * BQ:(qq + 1) * BQ]) + t_ref[hd, pl.ds(t0, B_WINDOW * BQ), :]
        s_ref[slot] = s
        smax_ref[slot] = jnp.max(s, axis=0, keepdims=True)

    def consume(slot, hd, qq):
        w0, _ = window(qq)
        p = jnp.exp2(s_ref[slot] - smax_ref[slot]).astype(BF16)
        acc = _dot(v_ref[0, hd, w0], p[0:BQ])
        for u in range(1, B_WINDOW):
            acc = acc + _dot(v_ref[0, hd, w0 + u], p[u * BQ:(u + 1) * BQ])
        o_ref[0, hd * HEAD_DIM:(hd + 1) * HEAD_DIM, qq * BQ:(qq + 1) * BQ] = (
            acc[0:HEAD_DIM] / acc[HEAD_DIM:HEAD_DIM + 1]).astype(BF16)

    chains = [(hd, qq) for qq in range(B_QCHUNKS) for hd in range(2)]
    fill(0, *chains[0])
    for n, chain in enumerate(chains):
        if n + 1 < len(chains):
            fill((n + 1) % 2, *chains[n + 1])
        consume(n % 2, *chain)


def _window_attn(qT, k, vT, table):
    B, _, _, S = qT.shape
    n_chunks = S // BQ
    assert n_chunks >= B_WINDOW and n_chunks % B_QCHUNKS == 0
    kernel = functools.partial(_window_attn_kernel, n_chunks=n_chunks)
    return pl.pallas_call(
        kernel,
        grid=(B, B_HEADS // 2, n_chunks // B_QCHUNKS),
        in_specs=[
            pl.BlockSpec((1, 2, 2 * HEAD_DIM, B_QCHUNKS * BQ), lambda b, h, i: (b, h, 0, i)),
            pl.BlockSpec((1, 1, S, 2 * HEAD_DIM), lambda b, h, i: (b, h, 0, 0)),
            pl.BlockSpec((1, 2, n_chunks, V_ROWS, BQ), lambda b, h, i: (b, h, 0, 0, 0)),
            pl.BlockSpec((2, (4 * B_REACH + 1) * BQ, BQ), lambda b, h, i: (h, 0, 0)),
        ],
        out_specs=pl.BlockSpec((1, 2 * HEAD_DIM, B_QCHUNKS * BQ), lambda b, h, i: (b, h, i)),
        out_shape=jax.ShapeDtypeStruct((B, B_W, S), BF16),
        scratch_shapes=[pltpu.VMEM((2, B_WINDOW * BQ, BQ), F32), pltpu.VMEM((2, 1, BQ), F32)],
        compiler_params=_params(("parallel", "parallel", "parallel")),
        name="window_attn",
    )(qT, k, vT, table)


def _out_proj_kernel(a_ref, b_ref, c_ref, w_ref, x_ref, g_ref, beta_ref, of_ref, ob_ref):
    mixT = jnp.concatenate([a_ref[0], b_ref[0], c_ref[0]], axis=0)
    w = w_ref[...]
    for r in range(0, TM, EPILOGUE_ROWS):
        rows = slice(r, r + EPILOGUE_ROWS)
        mix = mixT[:, rows].astype(F32).T.astype(BF16)
        x1 = _layer_norm(ALPHA * x_ref[rows, :] + _dot(mix, w), g_ref[...], beta_ref[...])
        of_ref[rows, :] = x1
        ob_ref[rows, :] = x1.astype(BF16)


def _out_proj(aT, bT, cT, w_o, x, g, beta, B, S):
    nt = S // TM
    n = B * S
    row = pl.BlockSpec((TM, D_MODEL), lambda b, i: (b * nt + i, 0))
    vec = pl.BlockSpec((1, D_MODEL), lambda b, i: (0, 0))
    featT = lambda a: pl.BlockSpec((1, a.shape[1], TM), lambda b, i: (b, 0, i))
    return pl.pallas_call(
        _out_proj_kernel,
        grid=(B, nt),
        in_specs=[featT(aT), featT(bT), featT(cT), pl.BlockSpec(w_o.shape, lambda b, i: (0, 0)), row, vec, vec],
        out_specs=[row, row],
        out_shape=[jax.ShapeDtypeStruct((n, D_MODEL), F32), jax.ShapeDtypeStruct((n, D_MODEL), BF16)],
        compiler_params=_params(("parallel", "parallel")),
        name="out_proj",
    )(aT, bT, cT, w_o, x, g.reshape(1, -1), beta.reshape(1, -1))


HALO = 16


def _ffn_up_kernel(xb_ref, prev_ref, next_ref, wg_ref, wv_ref, wc_ref, bc_ref, h_ref, *, n_tiles):
    i = pl.program_id(1)
    xb = xb_ref[...]
    wg = wg_ref[...]
    gate = _dot(xb, wg)
    val = _dot(xb, wv_ref[...])
    g_prev = _dot(prev_ref[...], wg)[HALO - 1:HALO]
    g_next = _dot(next_ref[...], wg)[0:1]
    g_prev = jnp.where(i > 0, g_prev, 0.0)
    g_next = jnp.where(i < n_tiles - 1, g_next, 0.0)
    row = lax.broadcasted_iota(jnp.int32, gate.shape, 0)
    up = jnp.where(row == 0, g_prev, pltpu.roll(gate, 1, axis=0))
    down = jnp.where(row == TM_UP - 1, g_next, pltpu.roll(gate, TM_UP - 1, axis=0))
    wc = wc_ref[...]
    conv = up * wc[0:1] + gate * wc[1:2] + down * wc[2:3] + bc_ref[...]
    act = 0.5 * conv * (1.0 + lax.erf(conv * (2.0 ** -0.5)))
    h_ref[...] = (act * val).astype(BF16)


def _ffn_down_kernel(h_ref, wd_ref, xf_ref, g_ref, beta_ref, of_ref, ob_ref):
    y = _dot(h_ref[...], wd_ref[...])
    x2 = _layer_norm(ALPHA * xf_ref[...] + y, g_ref[...], beta_ref[...])
    of_ref[...] = x2
    ob_ref[...] = x2.astype(BF16)


def _ffn(xb, xf, w_gate, w_val, w_conv, b_conv, w_down, g, beta, B, S):
    n = B * S
    nt = S // TM_UP
    hb = TM_UP // HALO
    last_halo = n // HALO - 1
    hid = pl.pallas_call(
        functools.partial(_ffn_up_kernel, n_tiles=nt),
        grid=(B, nt, D_FF // TF),
        in_specs=[
            pl.BlockSpec((TM_UP, D_MODEL), lambda b, i, j: (b * nt + i, 0)),
            pl.BlockSpec((HALO, D_MODEL), lambda b, i, j: (jnp.maximum((b * nt + i) * hb - 1, 0), 0)),
            pl.BlockSpec((HALO, D_MODEL), lambda b, i, j: (jnp.minimum((b * nt + i + 1) * hb, last_halo), 0)),
            pl.BlockSpec((D_MODEL, TF), lambda b, i, j: (0, j)),
            pl.BlockSpec((D_MODEL, TF), lambda b, i, j: (0, j)),
            pl.BlockSpec((CONV_W, TF), lambda b, i, j: (0, j)),
            pl.BlockSpec((1, TF), lambda b, i, j: (0, j)),
        ],
        out_specs=pl.BlockSpec((TM_UP, TF), lambda b, i, j: (b * nt + i, j)),
        out_shape=jax.ShapeDtypeStruct((n, D_FF), BF16),
        compiler_params=_params(("parallel", "parallel", "parallel")),
        name="ffn_up",
    )(xb, xb, xb, w_gate, w_val, w_conv, b_conv.reshape(1, -1))
    row = pl.BlockSpec((TM_DOWN, D_MODEL), lambda i: (i, 0))
    vec = pl.BlockSpec((1, D_MODEL), lambda i: (0, 0))
    return pl.pallas_call(
        _ffn_down_kernel,
        grid=(n // TM_DOWN,),
        in_specs=[
            pl.BlockSpec((TM_DOWN, D_FF), lambda i: (i, 0)),
            pl.BlockSpec((D_FF, D_MODEL), lambda i: (0, 0), pipeline_mode=pl.Buffered(1)),
            row, vec, vec,
        ],
        out_specs=[row, row],
        out_shape=[jax.ShapeDtypeStruct((n, D_MODEL), F32), jax.ShapeDtypeStruct((n, D_MODEL), BF16)],
        compiler_params=_params(("parallel",)),
        name="ffn_down",
    )(hid, w_down, xf, g.reshape(1, -1), beta.reshape(1, -1))


def _rope_tables(pos, dim):
    inv_freq = jnp.power(ROPE_THETA, -jnp.arange(0, dim, 2, dtype=F32) / dim)
    ang = pos.astype(F32)[:, None] * inv_freq[None, :]
    return jnp.cos(ang).T, jnp.sin(ang).T


def _position_tables(S):
    rows = S // GRID_W
    row = jnp.repeat(jnp.arange(rows, dtype=jnp.int32), GRID_W)
    col = jnp.tile(jnp.arange(GRID_W, dtype=jnp.int32), rows)
    t = jnp.arange(S, dtype=jnp.int32)
    cr, sr = _rope_tables(row, HEAD_DIM // 2)
    cc, sc = _rope_tables(col, HEAD_DIM // 2)
    c1, s1 = _rope_tables(t, C_ROPE)
    return (cr, sr, cc, sc), (c1, s1)


def _prepare_weights(w_in, w_uq, w_ukv, w_o, w_up, w_down):
    o = np.cumsum([0, A_W, A_KV_W, A_KV_W, B_W, B_W, B_W, C_Q_LORA, C_KV_LORA, C_ROPE])
    layers = []
    uq_cols = np.concatenate(
        [np.concatenate([np.arange(h * 96, h * 96 + C_NOPE) for h in range(C_HEADS)]),
         np.concatenate([np.arange(h * 96 + C_NOPE, (h + 1) * 96) for h in range(C_HEADS)])])
    ukv_cols = np.concatenate(
        [np.concatenate([np.arange(h * 128, h * 128 + C_NOPE) for h in range(C_HEADS)]),
         np.concatenate([np.arange(h * 128 + C_NOPE, (h + 1) * 128) for h in range(C_HEADS)])])
    for l in range(DEPTH):
        wi = w_in[l].astype(BF16)
        w_c = jnp.pad(wi[:, o[6]:o[9]], ((0, 0), (0, 128 - C_ROPE)))
        layers.append(dict(
            w_a=wi[:, o[0]:o[3]], w_b=wi[:, o[3]:o[6]], w_c=w_c,
            w_uq=w_uq[l].astype(BF16)[:, uq_cols], w_ukv=w_ukv[l].astype(BF16)[:, ukv_cols],
            w_o=w_o[l].astype(BF16),
            w_gate=w_up[l, :, :D_FF].astype(BF16), w_val=w_up[l, :, D_FF:].astype(BF16),
            w_down=w_down[l].astype(BF16)))
    return layers


def _trunk(x, weights, table, ln_emb_g, ln_emb_b, a_q_norm, a_k_norm, c_q_norm, c_kv_norm,
           ln1_g, ln1_b, w_conv, b_conv, ln2_g, ln2_b):
    B, S, _ = x.shape
    tabs_a, (c1, s1) = _position_tables(S)
    xf, xb = _embed_ln(x.reshape(B * S, D_MODEL), ln_emb_g, ln_emb_b)
    for l in range(DEPTH):
        w = weights[l]
        aq, ak, av = _proj_a(xb, w["w_a"], a_q_norm[l], a_k_norm[l], tabs_a, B, S)
        bq, bk, bv = _proj_b(xb, w["w_b"], B, S)
        cq, ck, cv = _proj_c(xb, w["w_c"], c_q_norm[l], c_kv_norm[l], w["w_uq"], w["w_ukv"], c1, s1, B, S)
        a_group, a_tq, a_unroll = _mixer_a_tiling(S // TK_A)
        a_out = _dense_attn(aq, ak.reshape(B, 1, S, A_KV_W), av, group=a_group, tq=a_tq, tk=TK_A,
                            unroll=a_unroll, name="attn_a")
        b_out = _window_attn(bq, bk, bv, table)
        c_out = _dense_attn(cq, ck, cv, group=1, tq=TQ_C, tk=TK_C, unroll=CHUNKS_PER_TRIP_C, name="attn_c")
        xf, xb = _out_proj(a_out, b_out, c_out, w["w_o"], xf, ln1_g[l], ln1_b[l], B, S)
        xf, xb = _ffn(xb, xf, w["w_gate"], w["w_val"], w_conv[l], b_conv[l], w["w_down"], ln2_g[l], ln2_b[l], B, S)
    return xf.reshape(B, S, D_MODEL)


def kernel(x_prompt, x_sample, ln_emb_g, ln_emb_b, w_in, a_q_norm, a_k_norm, c_q_norm, c_kv_norm, w_uq, w_ukv,
           w_o, ln1_g, ln1_b, w_up, w_conv, b_conv, w_down, ln2_g, ln2_b):
    weights = _prepare_weights(w_in, w_uq, w_ukv, w_o, w_up, w_down)
    table = _alibi_window_table()
    rest = (ln_emb_g, ln_emb_b, a_q_norm, a_k_norm, c_q_norm, c_kv_norm, ln1_g, ln1_b, w_conv, b_conv, ln2_g, ln2_b)
    return (_trunk(x_prompt, weights, table, *rest), _trunk(x_sample, weights, table, *rest))
```
